```python
import math
import jax, jax.numpy as jnp
from jax import lax
import numpy as np

D_MODEL = 1024
BATCH = 8
SEQ = 4096
DEPTH = 1

DA_HEADS = 8
DA_DH = 64
DA_QK = DA_HEADS * 2 * DA_DH
DA_V = DA_HEADS * 2 * DA_DH
DA_QBLOCK = 128
ROPE_THETA = 10000.0

GDN_HEADS = 8
GDN_DK = 128
GDN_DV = 128
GDN_QK = GDN_HEADS * GDN_DK
GDN_VW = GDN_HEADS * GDN_DV
GDN_CONV_DIM = 2 * GDN_QK + GDN_VW
CONV_K = 5
GDN_CHUNK = 64

N_EXPERTS = 16
EXPERT_FF = 2048
CAP_FACTOR = 2

NORM_EPS = 1e-6
IN_WIDTH = 2 * DA_QK + DA_V + GDN_CONV_DIM + GDN_VW + 4 * GDN_HEADS + 2 * D_MODEL

kernel_name = "hybrid_diffattn_gdn_ecmoe_encoder"


def rmsnorm(t, w, eps=NORM_EPS):
    t32 = t.astype(jnp.float32)
    y = t32 * lax.rsqrt(jnp.mean(t32 * t32, axis=-1, keepdims=True) + eps)
    return (y * w.astype(jnp.float32)).astype(t.dtype)


def l2norm(t, eps=1e-6):
    t32 = t.astype(jnp.float32)
    return t32 * lax.rsqrt(jnp.sum(t32 * t32, axis=-1, keepdims=True) + eps)


def rope_tables(seq, dim):
    inv = ROPE_THETA ** (-jnp.arange(0, dim, 2, dtype=jnp.float32) / dim)
    ang = jnp.arange(seq, dtype=jnp.float32)[:, None] * inv[None, :]
    ang = jnp.concatenate([ang, ang], axis=-1)
    return jnp.cos(ang), jnp.sin(ang)


def apply_rope(t, cos, sin):
    half = t.shape[-1] // 2
    t32 = t.astype(jnp.float32)
    rot = jnp.concatenate([-t32[..., half:], t32[..., :half]], axis=-1)
    shp = (1, t.shape[1]) + (1,) * (t.ndim - 3) + (t.shape[-1],)
    return (t32 * cos.reshape(shp) + rot * sin.reshape(shp)).astype(t.dtype)


def diff_attention(q, k, v, lam):
    b, s, h, _, dh = q.shape
    nb = s // DA_QBLOCK
    scale = dh ** -0.5
    qb = jnp.moveaxis(q.reshape(b, nb, DA_QBLOCK, h, 2, dh), 1, 0)

    def block(qblk):
        sc = jnp.einsum('bqhtd,bkhtd->bhtqk', qblk, k).astype(jnp.float32) * scale
        pr = jax.nn.softmax(sc, axis=-1)
        pd = pr[:, :, 0] - lam * pr[:, :, 1]
        return jnp.einsum('bhqk,bkhe->bqhe', pd.astype(v.dtype), v)

    o = lax.map(block, qb)
    return jnp.moveaxis(o, 0, 1).reshape(b, s, h, v.shape[-1])


def gated_delta_rule_chunked(q, k, v, g, beta):
    b, s, h, dk = q.shape
    dv = v.shape[-1]
    n = s // GDN_CHUNK

    def chunks(t):
        t = t.astype(jnp.float32).reshape((b, n, GDN_CHUNK, h) + t.shape[3:])
        return jnp.moveaxis(t, 3, 1)

    q = chunks(q) * dk ** -0.5
    k = chunks(k)
    v = chunks(v)
    g = jnp.cumsum(chunks(g), axis=-1)
    beta = chunks(beta)

    incl = jnp.tril(jnp.ones((GDN_CHUNK, GDN_CHUNK), dtype=bool))
    strict = jnp.tril(jnp.ones((GDN_CHUNK, GDN_CHUNK), dtype=bool), k=-1)
    gdiff = g[..., :, None] - g[..., None, :]
    decay = jnp.where(incl, jnp.exp(jnp.where(incl, gdiff, 0.0)), 0.0)

    k_beta = k * beta[..., None]
    a_mat = jnp.where(strict, jnp.einsum('bhnid,bhnjd->bhnij', k_beta, k) * decay, 0.0)
    t_mat = a_mat + jnp.eye(GDN_CHUNK, dtype=jnp.float32)
    rhs = jnp.concatenate([v * beta[..., None], k_beta * jnp.exp(g)[..., None]], axis=-1)
    sol = lax.linalg.triangular_solve(t_mat, rhs, left_side=True, lower=True,
                                      unit_diagonal=True)
    u, w = sol[..., :dv], sol[..., dv:]

    qk = jnp.where(incl, jnp.einsum('bhnid,bhnjd->bhnij', q, k) * decay, 0.0)
    q_dec = q * jnp.exp(g)[..., None]
    k_dec = k * jnp.exp(g[..., -1:] - g)[..., None]
    g_last = jnp.exp(g[..., -1])

    def step(state, inp):
        qk_c, qd_c, kd_c, u_c, w_c, gl_c = inp
        v_new = u_c - jnp.einsum('bhcd,bhde->bhce', w_c, state)
        o_c = (jnp.einsum('bhcd,bhde->bhce', qd_c, state)
               + jnp.einsum('bhij,bhje->bhie', qk_c, v_new))
        state = state * gl_c[..., None, None] + jnp.einsum('bhcd,bhce->bhde', kd_c, v_new)
        return state, o_c

    xs = tuple(jnp.moveaxis(t, 2, 0) for t in (qk, q_dec, k_dec, u, w, g_last))
    state0 = jnp.zeros((b, h, dk, dv), jnp.float32)
    _, o = lax.scan(step, state0, xs)
    o = jnp.moveaxis(o, 0, 2)
    return jnp.moveaxis(o, 1, 3).reshape(b, s, h, dv)


def centred_depthwise_conv(t, w):
    c = t.shape[-1]
    pad = CONV_K // 2
    return lax.conv_general_dilated(
        t, w[:, None, :].astype(t.dtype), window_strides=(1,), padding=[(pad, pad)],
        dimension_numbers=('NWC', 'WIO', 'NWC'), feature_group_count=c)


def expert_choice_ffn(h, w_router, w_gate, w_up, w_down):
    b, s, d = h.shape
    cap = CAP_FACTOR * s // N_EXPERTS
    logits = jnp.einsum('bsd,de->bse', h, w_router).astype(jnp.float32)
    aff = jax.nn.softmax(logits, axis=-1)
    top_val, top_idx = lax.top_k(jnp.swapaxes(aff, 1, 2), cap)
    xs = jax.vmap(lambda hb, ib: hb[ib])(h, top_idx)
    gt = jnp.einsum('becd,edf->becf', xs, w_gate)
    up = jnp.einsum('becd,edf->becf', xs, w_up)
    ye = jnp.einsum('becf,efd->becd', jax.nn.silu(gt) * up, w_down)
    ye = ye * top_val[..., None].astype(ye.dtype)

    def scatter(yb, ib):
        return jnp.zeros((s, d), yb.dtype).at[ib.reshape(-1)].add(yb.reshape(-1, d))

    return jax.vmap(scatter)(ye, top_idx)


def setup_inputs(seed: int = 0) -> dict:
    key = jax.random.key(seed)
    ks = jax.random.split(key, 32)
    f32 = jnp.float32

    def nrm(k, shape, scale):
        return jax.random.normal(k, shape, f32) * scale

    def gain(k, n):
        return 1.0 + 0.02 * jax.random.normal(k, (DEPTH, n), f32)

    def a_log(k):
        return jnp.log(jax.random.uniform(k, (DEPTH, GDN_HEADS), f32, 1.0, 16.0))

    def dt_bias(k):
        dt = jnp.exp(jax.random.uniform(k, (DEPTH, GDN_HEADS), f32,
                                        math.log(1e-3), math.log(1e-1)))
        return dt + jnp.log(-jnp.expm1(-dt))

    return {
        "x": nrm(ks[0], (BATCH, SEQ, D_MODEL), 1.0),
        "norm1_w": gain(ks[1], D_MODEL),
        "w_in": nrm(ks[2], (DEPTH, D_MODEL, IN_WIDTH), D_MODEL ** -0.5),
        "conv_w": nrm(ks[3], (DEPTH, CONV_K, GDN_CONV_DIM), CONV_K ** -0.5),
        "a_log_fwd": a_log(ks[4]),
        "dt_bias_fwd": dt_bias(ks[5]),
        "a_log_bwd": a_log(ks[6]),
        "dt_bias_bwd": dt_bias(ks[7]),
        "gdn_norm_w": gain(ks[8], GDN_DV),
        "lambda_q1": nrm(ks[9], (DEPTH, DA_DH), 0.1),
        "lambda_k1": nrm(ks[10], (DEPTH, DA_DH), 0.1),
        "lambda_q2": nrm(ks[11], (DEPTH, DA_DH), 0.1),
        "lambda_k2": nrm(ks[12], (DEPTH, DA_DH), 0.1),
        "subln_w": gain(ks[13], 2 * DA_DH),
        "w_proj_attn": nrm(ks[14], (DEPTH, DA_V, D_MODEL), DA_V ** -0.5),
        "w_proj_gdn": nrm(ks[15], (DEPTH, GDN_VW, D_MODEL), GDN_VW ** -0.5),
        "w_out": nrm(ks[16], (DEPTH, D_MODEL, D_MODEL), D_MODEL ** -0.5),
        "norm2_w": gain(ks[17], D_MODEL),
        "w_router": nrm(ks[18], (DEPTH, D_MODEL, N_EXPERTS), D_MODEL ** -0.5),
        "w_gate": nrm(ks[19], (DEPTH, N_EXPERTS, D_MODEL, EXPERT_FF), D_MODEL ** -0.5),
        "w_up": nrm(ks[20], (DEPTH, N_EXPERTS, D_MODEL, EXPERT_FF), D_MODEL ** -0.5),
        "w_down": nrm(ks[21], (DEPTH, N_EXPERTS, EXPERT_FF, D_MODEL), EXPERT_FF ** -0.5),
        "norm_f_w": 1.0 + 0.02 * jax.random.normal(ks[22], (D_MODEL,), f32),
    }


def reference(x, norm1_w, w_in, conv_w, a_log_fwd, dt_bias_fwd, a_log_bwd, dt_bias_bwd,
              gdn_norm_w, lambda_q1, lambda_k1, lambda_q2, lambda_k2, subln_w,
              w_proj_attn, w_proj_gdn, w_out, norm2_w, w_router, w_gate, w_up, w_down,
              norm_f_w):
    b, s, _ = x.shape
    cos, sin = rope_tables(s, DA_DH)
    sizes = [DA_QK, DA_QK, DA_V, GDN_CONV_DIM, GDN_VW,
             GDN_HEADS, GDN_HEADS, GDN_HEADS, GDN_HEADS, D_MODEL, D_MODEL]
    split_at = [int(v) for v in np.cumsum(sizes)[:-1]]

    for l in range(DEPTH):
        lambda_init = 0.8 - 0.6 * math.exp(-0.3 * l)
        h = rmsnorm(x, norm1_w[l])
        p = jnp.einsum('bsd,dn->bsn', h, w_in[l])
        (qa, ka, va, qkv_g, z, a_f, a_b, b_f, b_b, gate_a, gate_g) = jnp.split(p, split_at, axis=-1)

        qa = apply_rope(qa.reshape(b, s, DA_HEADS, 2, DA_DH), cos, sin)
        ka = apply_rope(ka.reshape(b, s, DA_HEADS, 2, DA_DH), cos, sin)
        va = va.reshape(b, s, DA_HEADS, 2 * DA_DH)
        lam = (jnp.exp(jnp.sum(lambda_q1[l].astype(jnp.float32) * lambda_k1[l].astype(jnp.float32)))
               - jnp.exp(jnp.sum(lambda_q2[l].astype(jnp.float32) * lambda_k2[l].astype(jnp.float32)))
               + lambda_init)
        oa = diff_attention(qa, ka, va, lam)
        oa = rmsnorm(oa, subln_w[l]) * (1.0 - lambda_init)
        ya = jnp.einsum('bse,ed->bsd', oa.reshape(b, s, DA_V).astype(x.dtype), w_proj_attn[l])

        qkv_g = jax.nn.silu(centred_depthwise_conv(qkv_g, conv_w[l]))
        qg, kg, vg = jnp.split(qkv_g, [GDN_QK, 2 * GDN_QK], axis=-1)
        qg = l2norm(qg.reshape(b, s, GDN_HEADS, GDN_DK))
        kg = l2norm(kg.reshape(b, s, GDN_HEADS, GDN_DK))
        vg = vg.reshape(b, s, GDN_HEADS, GDN_DV).astype(jnp.float32)
        g_f = -jnp.exp(a_log_fwd[l].astype(jnp.float32)) * jax.nn.softplus(
            (a_f + dt_bias_fwd[l]).astype(jnp.float32))
        g_b = -jnp.exp(a_log_bwd[l].astype(jnp.float32)) * jax.nn.softplus(
            (a_b + dt_bias_bwd[l]).astype(jnp.float32))
        beta_f = jax.nn.sigmoid(b_f.astype(jnp.float32))
        beta_b = jax.nn.sigmoid(b_b.astype(jnp.float32))
        o_fwd = gated_delta_rule_chunked(qg, kg, vg, g_f, beta_f)
        flip = lambda t: jnp.flip(t, axis=1)
        o_bwd = flip(gated_delta_rule_chunked(flip(qg), flip(kg), flip(vg), flip(g_b), flip(beta_b)))
        og = rmsnorm(o_fwd + o_bwd, gdn_norm_w[l])
        og = og * jax.nn.silu(z.reshape(b, s, GDN_HEADS, GDN_DV).astype(jnp.float32))
        yg = jnp.einsum('bse,ed->bsd', og.reshape(b, s, GDN_VW).astype(x.dtype), w_proj_gdn[l])

        merged = jax.nn.sigmoid(gate_a) * ya + jax.nn.sigmoid(gate_g) * yg
        x = x + jnp.einsum('bsd,de->bse', merged, w_out[l])

        h2 = rmsnorm(x, norm2_w[l])
        x = x + expert_choice_ffn(h2, w_router[l], w_gate[l], w_up[l], w_down[l])

    return rmsnorm(x, norm_f_w)
```

```python
import functools
import math

import jax
import jax.numpy as jnp
from jax import lax
from jax.experimental import pallas as pl
from jax.experimental.pallas import tpu as pltpu

F32 = jnp.float32
BF16 = jnp.bfloat16
HIGHEST = lax.Precision.HIGHEST

D_MODEL = 1024
N_HEADS = 8
HEAD_W = 128
DA_DH = 64
ROPE_THETA = 10000.0
CONV_K = 5
GDN_CHUNK = 64
N_EXPERTS = 16
EXPERT_FF = 2048
CAP_FACTOR = 2
NORM_EPS = 1e-6
L2_EPS = 1e-6
LAMBDA_INIT = 0.8 - 0.6 * math.exp(-0.3 * 0)

QA_BLK, KA_BLK, VA_BLK = 0, 8, 16
GQ_BLK, GK_BLK, GV_BLK, Z_BLK = 24, 32, 40, 48
GATE_A_BLK1024, GATE_G_BLK1024 = 7, 8
MAIN_W = 9 * D_MODEL

VMEM_LIMIT = 56 * 1024 * 1024


def _cparams(sem, vmem=VMEM_LIMIT):
    return pltpu.CompilerParams(dimension_semantics=sem, vmem_limit_bytes=vmem)


def _dot(a, b, precision=None):
    return jnp.dot(a, b, preferred_element_type=F32, precision=precision)


def _dot_nt(a, b, precision=None):
    return lax.dot_general(a, b, (((1,), (1,)), ((), ())), preferred_element_type=F32,
                           precision=precision)


def _dot_tn(a, b, precision=None):
    return lax.dot_general(a, b, (((0,), (0,)), ((), ())), preferred_element_type=F32,
                           precision=precision)


def _rms(t, w, eps=NORM_EPS):
    return t * lax.rsqrt(jnp.mean(t * t, axis=-1, keepdims=True) + eps) * w


def _inproj_kernel(x_ref, nw_ref, wm_ref, ws_ref, om_ref, os_ref, h_scr):
    @pl.when(pl.program_id(2) == 0)
    def _():
        h = _rms(x_ref[0], nw_ref[...]).astype(BF16)
        h_scr[...] = h
        os_ref[0] = _dot(h, ws_ref[...])

    om_ref[0] = _dot(h_scr[...], wm_ref[...])


def _in_proj(x, nw, w_main, w_small):
    b, s, d = x.shape
    tm = min(1024, s)
    tn = 1024
    return pl.pallas_call(
        _inproj_kernel,
        grid=(b, s // tm, MAIN_W // tn),
        in_specs=[
            pl.BlockSpec((1, tm, d), lambda i, m, n: (i, m, 0)),
            pl.BlockSpec((1, d), lambda i, m, n: (0, 0)),
            pl.BlockSpec((d, tn), lambda i, m, n: (0, n)),
            pl.BlockSpec((d, HEAD_W), lambda i, m, n: (0, 0)),
        ],
        out_specs=[
            pl.BlockSpec((1, tm, tn), lambda i, m, n: (i, m, n)),
            pl.BlockSpec((1, tm, HEAD_W), lambda i, m, n: (i, m, 0)),
        ],
        out_shape=[
            jax.ShapeDtypeStruct((b, s, MAIN_W), F32),
            jax.ShapeDtypeStruct((b, s, HEAD_W), F32),
        ],
        scratch_shapes=[pltpu.VMEM((tm, d), BF16)],
        compiler_params=_cparams(("parallel", "parallel", "arbitrary")),
        name="in_proj",
    )(x, nw, w_main, w_small)


def _attn_kernel(lq1_ref, lk1_ref, lq2_ref, lk2_ref, q_ref, k_ref, v_ref, cos_ref, sin_ref,
                 subw_ref, o_ref, kr_scr, vb_scr, *, tq):
    qi = pl.program_id(2)
    lane = lax.broadcasted_iota(jnp.int32, (1, HEAD_W), 1)
    first_half = (lane % DA_DH) < (DA_DH // 2)

    def rope(t, cos, sin_signed):
        rot = jnp.where(first_half, pltpu.roll(t, HEAD_W - DA_DH // 2, 1),
                        pltpu.roll(t, DA_DH // 2, 1))
        return t * cos + rot * sin_signed

    @pl.when(qi == 0)
    def _():
        kr_scr[...] = rope(k_ref[0], cos_ref[...], sin_ref[...]).astype(BF16)
        vb_scr[...] = v_ref[0].astype(BF16)

    lam = (jnp.exp(jnp.sum(lq1_ref[...] * lk1_ref[...], axis=-1, keepdims=True))
           - jnp.exp(jnp.sum(lq2_ref[...] * lk2_ref[...], axis=-1, keepdims=True))
           + LAMBDA_INIT)

    r0 = pl.multiple_of(qi * tq, tq)
    q = rope(q_ref[0], cos_ref[pl.ds(r0, tq), :], sin_ref[pl.ds(r0, tq), :]) * (DA_DH ** -0.5)
    lo = lane < DA_DH
    q2 = jnp.concatenate([jnp.where(lo, q, 0.0), jnp.where(lo, 0.0, q)], axis=0).astype(BF16)
    s = _dot_nt(q2, kr_scr[...])
    m = jnp.max(s, axis=-1, keepdims=True)
    p = jnp.exp(s - m)
    p = p * (1.0 / jnp.sum(p, axis=-1, keepdims=True))
    pd = p[:tq] - lam * p[tq:]
    o = _dot(pd.astype(BF16), vb_scr[...])
    o = _rms(o, subw_ref[...]) * (1.0 - LAMBDA_INIT)
    o_ref[0] = o.astype(o_ref.dtype)


def _attention(p_main, cos, sin_signed, lq1, lk1, lq2, lk2, subw):
    b, s, _ = p_main.shape
    tq = min(256, s)
    vec = pl.BlockSpec((1, DA_DH), lambda i, h, q: (0, 0))
    return pl.pallas_call(
        functools.partial(_attn_kernel, tq=tq),
        grid=(b, N_HEADS, s // tq),
        in_specs=[
            vec, vec, vec, vec,
            pl.BlockSpec((1, tq, HEAD_W), lambda i, h, q: (i, q, QA_BLK + h)),
            pl.BlockSpec((1, s, HEAD_W), lambda i, h, q: (i, 0, KA_BLK + h)),
            pl.BlockSpec((1, s, HEAD_W), lambda i, h, q: (i, 0, VA_BLK + h)),
            pl.BlockSpec((s, HEAD_W), lambda i, h, q: (0, 0)),
            pl.BlockSpec((s, HEAD_W), lambda i, h, q: (0, 0)),
            pl.BlockSpec((1, HEAD_W), lambda i, h, q: (0, 0)),
        ],
        out_specs=pl.BlockSpec((1, tq, HEAD_W), lambda i, h, q: (i, q, h)),
        out_shape=jax.ShapeDtypeStruct((b, s, N_HEADS * HEAD_W), BF16),
        scratch_shapes=[pltpu.VMEM((s, HEAD_W), BF16), pltpu.VMEM((s, HEAD_W), BF16)],
        compiler_params=_cparams(("parallel", "parallel", "arbitrary")),
        name="diff_attn",
    )(lq1, lk1, lq2, lk2, p_main, p_main, p_main, cos, sin_signed, subw)


def _softplus(x):
    return jnp.maximum(x, 0.0) + jnp.log1p(jnp.exp(-jnp.abs(x)))


def _gdn_kernel(alf_ref, dtf_ref, alb_ref, dtb_ref,
                q_ref, k_ref, v_ref, z_ref, sm_ref, cwq_ref, cwk_ref, cwv_ref, nw_ref,
                o_ref,
                pad_scr, q_scr, k_scr, v_scr, u_scr, w_scr, qd_scr, kd_scr, qk_scr, gl_scr,
                o_scr, *, seq):
    C = GDN_CHUNK
    n_chunks = seq // C
    h = pl.program_id(1)
    halo = 8
    conv_rows = min(512, seq)

    zeros_halo = jnp.zeros((halo, HEAD_W), F32)
    pad_scr[0:halo, :] = zeros_halo
    pad_scr[halo + seq:2 * halo + seq, :] = zeros_halo

    def conv_silu(src_ref, cw_ref, dst_scr, normalize):
        pad_scr[halo:halo + seq, :] = src_ref[0]
        cw = cw_ref[...]
        for r0 in range(0, seq, conv_rows):
            acc = None
            for t in range(CONV_K):
                start = halo + r0 + t - CONV_K // 2
                term = cw[t:t + 1, :] * pad_scr[start:start + conv_rows, :]
                acc = term if acc is None else acc + term
            y = acc * jax.nn.sigmoid(acc)
            if normalize:
                y = y * lax.rsqrt(jnp.sum(y * y, axis=-1, keepdims=True) + L2_EPS)
            dst_scr[r0:r0 + conv_rows, :] = y

    conv_silu(q_ref, cwq_ref, q_scr, True)
    conv_silu(k_ref, cwk_ref, k_scr, True)
    conv_silu(v_ref, cwv_ref, v_scr, False)

    ri = lax.broadcasted_iota(jnp.int32, (C, C), 0)
    ci = lax.broadcasted_iota(jnp.int32, (C, C), 1)
    eye = ri == ci
    eye_f = eye.astype(F32)
    ones_cc = jnp.ones((C, C), F32)
    incl = (ci <= ri, ci >= ri)
    strict = (ci < ri, ci > ri)
    lane = lax.broadcasted_iota(jnp.int32, (1, HEAD_W), 1)
    neg_decay_rate = (-jnp.exp(jnp.full((1, 1), alf_ref[h], F32)),
                      -jnp.exp(jnp.full((1, 1), alb_ref[h], F32)))
    dt_bias = (dtf_ref[h], dtb_ref[h])
    dk_scale = HEAD_W ** -0.5

    def lane_pick(t, idx):
        return jnp.sum(jnp.where(lane == idx, t, 0.0), axis=-1, keepdims=True)

    def unit_tri_inverse(a):
        inv = eye_f - a
        pw = a
        for _ in range(int(math.log2(C)) - 1):
            pw = _dot(pw, pw, HIGHEST)
            inv = inv + _dot(inv, pw, HIGHEST)
        return inv

    def precompute(c, carry):
        r0 = pl.multiple_of(c * C, C)
        rows = pl.ds(r0, C)
        q = q_scr[rows, :] * dk_scale
        k = k_scr[rows, :]
        v = v_scr[rows, :]
        sm = sm_ref[0, rows, :]
        k16 = k.astype(BF16)
        qk_raw = _dot_nt(q.astype(BF16), k16)
        for d in range(2):
            a_logit = lane_pick(sm, d * N_HEADS + h)
            b_logit = lane_pick(sm, (2 + d) * N_HEADS + h)
            g = neg_decay_rate[d] * _softplus(a_logit + dt_bias[d])
            beta = jax.nn.sigmoid(b_logit)
            g_rep = jnp.broadcast_to(g, (C, HEAD_W))
            cum = _dot(incl[d].astype(F32), g_rep, HIGHEST)
            cum_i = cum[:, :C]
            cum_j = _dot(ones_cc, jnp.where(eye, cum_i, 0.0), HIGHEST)
            decay = jnp.where(incl[d], jnp.exp(jnp.where(incl[d], cum_i - cum_j, 0.0)), 0.0)
            k_beta = k * beta
            a_mat = jnp.where(strict[d], _dot_nt(k_beta.astype(BF16), k16) * decay, 0.0)
            t_inv = unit_tri_inverse(a_mat)
            e_cum = jnp.exp(cum)
            u = _dot(t_inv, v * beta, HIGHEST)
            w = _dot(t_inv, k_beta * e_cum, HIGHEST)
            cum_end = cum[C - 1:C, :] if d == 0 else cum[0:1, :]
            u_scr[d, rows, :] = u
            w_scr[d, rows, :] = w.astype(BF16)
            qd_scr[d, rows, :] = (q * e_cum).astype(BF16)
            kd_scr[d, rows, :] = (k * jnp.exp(cum_end - cum)).astype(BF16)
            qk_scr[d, rows, :] = jnp.where(incl[d], qk_raw * decay, 0.0).astype(BF16)
            gl_scr[d, pl.ds(pl.multiple_of(c * 8, 8), 8), :] = jnp.broadcast_to(
                jnp.exp(cum_end), (8, HEAD_W))
        return carry

    lax.fori_loop(0, n_chunks, precompute, 0)

    def scan_step(t, states):
        new_states = []
        for d in range(2):
            n = t if d == 0 else n_chunks - 1 - t
            rows = pl.ds(pl.multiple_of(n * C, C), C)
            st = states[d]
            st16 = st.astype(BF16)
            v_new = u_scr[d, rows, :] - _dot(w_scr[d, rows, :], st16)
            vn16 = v_new.astype(BF16)
            o = _dot(qd_scr[d, rows, :], st16) + _dot(qk_scr[d, rows, :], vn16)
            gl = gl_scr[d, pl.ds(pl.multiple_of(n * 8, 8), 1), :]
            new_states.append(st * gl + _dot_tn(kd_scr[d, rows, :], vn16))
            o_scr[d, rows, :] = o
        return tuple(new_states)

    zero_state = jnp.zeros((HEAD_W, HEAD_W), F32)
    lax.fori_loop(0, n_chunks, scan_step, (zero_state, zero_state))

    for r0 in range(0, seq, conv_rows):
        sl = slice(r0, r0 + conv_rows)
        o = _rms(o_scr[0, sl, :] + o_scr[1, sl, :], nw_ref[...])
        z = z_ref[0, sl, :]
        o_ref[0, sl, :] = (o * (z * jax.nn.sigmoid(z))).astype(o_ref.dtype)


def _gdn(p_main, p_small, conv_w, alf, dtf, alb, dtb, nw):
    b, s, _ = p_main.shape
    smem = pl.BlockSpec(memory_space=pltpu.SMEM)

    def col(blk):
        return pl.BlockSpec((1, s, HEAD_W), lambda i, h: (i, 0, blk + h))

    def cw(blk):
        return pl.BlockSpec((CONV_K, HEAD_W), lambda i, h: (0, blk + h))

    return pl.pallas_call(
        functools.partial(_gdn_kernel, seq=s),
        grid=(b, N_HEADS),
        in_specs=[
            smem, smem, smem, smem,
            col(GQ_BLK), col(GK_BLK), col(GV_BLK), col(Z_BLK),
            pl.BlockSpec((1, s, HEAD_W), lambda i, h: (i, 0, 0)),
            cw(0), cw(N_HEADS), cw(2 * N_HEADS),
            pl.BlockSpec((1, HEAD_W), lambda i, h: (0, 0)),
        ],
        out_specs=pl.BlockSpec((1, s, HEAD_W), lambda i, h: (i, 0, h)),
        out_shape=jax.ShapeDtypeStruct((b, s, N_HEADS * HEAD_W), BF16),
        scratch_shapes=[
            pltpu.VMEM((s + 16, HEAD_W), F32),
            pltpu.VMEM((s, HEAD_W), F32),
            pltpu.VMEM((s, HEAD_W), F32),
            pltpu.VMEM((s, HEAD_W), F32),
            pltpu.VMEM((2, s, HEAD_W), F32),
            pltpu.VMEM((2, s, HEAD_W), BF16),
            pltpu.VMEM((2, s, HEAD_W), BF16),
            pltpu.VMEM((2, s, HEAD_W), BF16),
            pltpu.VMEM((2, s, GDN_CHUNK), BF16),
            pltpu.VMEM((2, (s // GDN_CHUNK) * 8, HEAD_W), F32),
            pltpu.VMEM((2, s, HEAD_W), F32),
        ],
        compiler_params=_cparams(("parallel", "arbitrary")),
        name="gdn",
    )(alf, dtf, alb, dtb, p_main, p_main, p_main, p_main, p_small, conv_w, conv_w, conv_w, nw)


def _merge_kernel(x_ref, oa_ref, og_ref, ga_ref, gg_ref, wpa_ref, wpg_ref, wo_ref, n2w_ref,
                  wr_ref, x2_ref, aff_ref):
    ya = _dot(oa_ref[0], wpa_ref[...])
    yg = _dot(og_ref[0], wpg_ref[...])
    merged = jax.nn.sigmoid(ga_ref[0]) * ya + jax.nn.sigmoid(gg_ref[0]) * yg
    x2 = x_ref[0] + _dot(merged.astype(BF16), wo_ref[...])
    x2_ref[0] = x2
    h2 = _rms(x2, n2w_ref[...])
    logits = _dot(h2, wr_ref[...], HIGHEST)
    lane = lax.broadcasted_iota(jnp.int32, logits.shape, 1)
    valid = lane < N_EXPERTS
    logits = jnp.where(valid, logits, -1e30)
    e = jnp.exp(logits - jnp.max(logits, axis=-1, keepdims=True))
    e = jnp.where(valid, e, 0.0)
    aff_ref[0] = e * (1.0 / jnp.sum(e, axis=-1, keepdims=True))


def _merge(x, oa, og, p_main, wpa, wpg, wo, n2w, wr):
    b, s, d = x.shape
    tm = min(512, s)
    tok = lambda i, m: (i, m, 0)
    const = lambda i, m: (0, 0)
    return pl.pallas_call(
        _merge_kernel,
        grid=(b, s // tm),
        in_specs=[
            pl.BlockSpec((1, tm, d), tok),
            pl.BlockSpec((1, tm, d), tok),
            pl.BlockSpec((1, tm, d), tok),
            pl.BlockSpec((1, tm, d), lambda i, m: (i, m, GATE_A_BLK1024)),
            pl.BlockSpec((1, tm, d), lambda i, m: (i, m, GATE_G_BLK1024)),
            pl.BlockSpec((d, d), const),
            pl.BlockSpec((d, d), const),
            pl.BlockSpec((d, d), const),
            pl.BlockSpec((1, d), const),
            pl.BlockSpec((d, HEAD_W), const),
        ],
        out_specs=[pl.BlockSpec((1, tm, d), tok), pl.BlockSpec((1, tm, HEAD_W), tok)],
        out_shape=[jax.ShapeDtypeStruct((b, s, d), F32),
                   jax.ShapeDtypeStruct((b, s, HEAD_W), F32)],
        compiler_params=_cparams(("parallel", "parallel")),
        name="merge",
    )(x, oa, og, p_main, p_main, wpa, wpg, wo, n2w, wr)


def _route_kernel(aff_ref, idx_ref, val_ref, csm_scr, *, seq, cap):
    blk = 128
    bits = lax.bitcast_convert_type(aff_ref[0], jnp.int32)

    def bisect(i, cur):
        cand = cur | jnp.left_shift(jnp.int32(1), 30 - i)
        cnt = jnp.sum((bits >= cand).astype(F32), axis=0, keepdims=True)
        return jnp.where(cnt >= cap, cand, cur)

    thr = lax.fori_loop(0, 31, bisect, jnp.zeros((1, HEAD_W), jnp.int32))
    n_gt = jnp.sum((bits > thr).astype(F32), axis=0, keepdims=True)
    n_ties = cap - n_gt

    ri = lax.broadcasted_iota(jnp.int32, (blk, blk), 0)
    ci = lax.broadcasted_iota(jnp.int32, (blk, blk), 1)
    ltri = (ci <= ri).astype(BF16)

    def select(j, carry):
        tie_base, sel_base = carry
        rows = pl.ds(pl.multiple_of(j * blk, blk), blk)
        bj = lax.bitcast_convert_type(aff_ref[0, rows, :], jnp.int32)
        eq = jnp.where(bj == thr, 1.0, 0.0)
        tie_rank = _dot(ltri, eq.astype(BF16)) + tie_base - eq
        sel = jnp.where(bj > thr, 1.0, eq * jnp.where(tie_rank < n_ties, 1.0, 0.0))
        cs = _dot(ltri, sel.astype(BF16)) + sel_base
        csm_scr[rows, :] = cs * sel
        return (tie_base + jnp.sum(eq, axis=0, keepdims=True),
                sel_base + jnp.sum(sel, axis=0, keepdims=True))

    zero = jnp.zeros((1, HEAD_W), F32)
    lax.fori_loop(0, seq // blk, select, (zero, zero))

    rb = min(512, seq)
    slot = lax.broadcasted_iota(jnp.int32, (1, cap), 1).astype(F32) + 1.0
    for e in range(N_EXPERTS):
        def gather_slots(j, carry):
            idx_acc, val_acc = carry
            r0 = pl.multiple_of(j * rb, rb)
            rows = pl.ds(r0, rb)
            onehot = csm_scr[rows, e:e + 1] == slot
            tok = (lax.broadcasted_iota(jnp.int32, (rb, 1), 0) + r0).astype(F32)
            idx_acc = idx_acc + jnp.sum(jnp.where(onehot, tok, 0.0), axis=0, keepdims=True)
            val_acc = val_acc + jnp.sum(jnp.where(onehot, aff_ref[0, rows, e:e + 1], 0.0),
                                        axis=0, keepdims=True)
            return idx_acc, val_acc

        zc = jnp.zeros((1, cap), F32)
        idx_e, val_e = lax.fori_loop(0, seq // rb, gather_slots, (zc, zc))
        idx_ref[0, e:e + 1, :] = idx_e.astype(jnp.int32)
        val_ref[0, e:e + 1, :] = val_e


def _route(aff, cap):
    b, s, _ = aff.shape
    return pl.pallas_call(
        functools.partial(_route_kernel, seq=s, cap=cap),
        grid=(b,),
        in_specs=[pl.BlockSpec((1, s, HEAD_W), lambda i: (i, 0, 0))],
        out_specs=[pl.BlockSpec((1, N_EXPERTS, cap), lambda i: (i, 0, 0)),
                   pl.BlockSpec((1, N_EXPERTS, cap), lambda i: (i, 0, 0))],
        out_shape=[jax.ShapeDtypeStruct((b, N_EXPERTS, cap), jnp.int32),
                   jax.ShapeDtypeStruct((b, N_EXPERTS, cap), F32)],
        scratch_shapes=[pltpu.VMEM((s, HEAD_W), F32)],
        compiler_params=_cparams(("parallel",)),
        name="route",
    )(aff)


def _moe_kernel(idx_ref, val_ref, x2_hbm, n2w_ref, wg_ref, wu_ref, wd_ref, nfw_ref, out_hbm,
                x2_v, acc_v, xg_v, xs_v, y_v, in_sem, out_sem, *, seq, cap, n_ff):
    b = pl.program_id(0)
    e = pl.program_id(1)
    f = pl.program_id(2)
    slot_row = b * N_EXPERTS + e

    @pl.when((e == 0) & (f == 0))
    def _():
        cp = pltpu.make_async_copy(x2_hbm.at[b], x2_v, in_sem)
        cp.start()
        acc_v[...] = jnp.zeros_like(acc_v)
        cp.wait()

    @pl.when(f == 0)
    def _():
        def gather(c, carry):
            tok = idx_ref[slot_row, c]
            xg_v[pl.ds(c, 1), :] = x2_v[pl.ds(tok, 1), :]
            return carry

        lax.fori_loop(0, cap, gather, 0, unroll=8)
        xs_v[...] = _rms(xg_v[...], n2w_ref[...]).astype(BF16)

    xs = xs_v[...]
    gate = _dot(xs, wg_ref[0])
    up = _dot(xs, wu_ref[0])
    act = (gate * jax.nn.sigmoid(gate) * up).astype(BF16)
    y_part = _dot(act, wd_ref[0])

    @pl.when(f == 0)
    def _():
        y_v[...] = y_part

    @pl.when(f > 0)
    def _():
        y_v[...] += y_part

    @pl.when(f == n_ff - 1)
    def _():
        def scatter(c, carry):
            tok = idx_ref[slot_row, c]
            wgt = val_ref[slot_row, c]
            acc_v[pl.ds(tok, 1), :] += y_v[pl.ds(c, 1), :] * wgt
            return carry

        lax.fori_loop(0, cap, scatter, 0, unroll=8)

    @pl.when((e == N_EXPERTS - 1) & (f == n_ff - 1))
    def _():
        rb = min(512, seq)
        for r0 in range(0, seq, rb):
            sl = slice(r0, r0 + rb)
            acc_v[sl, :] = _rms(x2_v[sl, :] + acc_v[sl, :], nfw_ref[...])
        cp = pltpu.make_async_copy(acc_v, out_hbm.at[b], out_sem)
        cp.start()
        cp.wait()


def _moe(x2, idx, val, n2w, wg, wu, wd, nfw):
    b, s, d = x2.shape
    cap = idx.shape[-1]
    tf = 1024
    n_ff = EXPERT_FF // tf
    grid_spec = pltpu.PrefetchScalarGridSpec(
        num_scalar_prefetch=2,
        grid=(b, N_EXPERTS, n_ff),
        in_specs=[
            pl.BlockSpec(memory_space=pl.ANY),
            pl.BlockSpec((1, d), lambda i, e, f, *_: (0, 0)),
            pl.BlockSpec((1, d, tf), lambda i, e, f, *_: (e, 0, f)),
            pl.BlockSpec((1, d, tf), lambda i, e, f, *_: (e, 0, f)),
            pl.BlockSpec((1, tf, d), lambda i, e, f, *_: (e, f, 0)),
            pl.BlockSpec((1, d), lambda i, e, f, *_: (0, 0)),
        ],
        out_specs=pl.BlockSpec(memory_space=pl.ANY),
        scratch_shapes=[
            pltpu.VMEM((s, d), F32),
            pltpu.VMEM((s, d), F32),
            pltpu.VMEM((cap, d), F32),
            pltpu.VMEM((cap, d), BF16),
            pltpu.VMEM((cap, d), F32),
            pltpu.SemaphoreType.DMA,
            pltpu.SemaphoreType.DMA,
        ],
    )
    return pl.pallas_call(
        functools.partial(_moe_kernel, seq=s, cap=cap, n_ff=n_ff),
        grid_spec=grid_spec,
        out_shape=jax.ShapeDtypeStruct((b, s, d), F32),
        compiler_params=_cparams(("arbitrary", "arbitrary", "arbitrary"), vmem=60 * 1024 * 1024),
        name="moe",
    )(idx.reshape(b * N_EXPERTS, cap), val.reshape(b * N_EXPERTS, cap), x2, n2w, wg, wu, wd, nfw)


def _rope_tables(seq):
    inv = ROPE_THETA ** (-jnp.arange(0, DA_DH, 2, dtype=F32) / DA_DH)
    ang = jnp.arange(seq, dtype=F32)[:, None] * inv[None, :]
    cos, sin = jnp.cos(ang), jnp.sin(ang)
    return (jnp.concatenate([cos, cos, cos, cos], axis=-1),
            jnp.concatenate([-sin, sin, -sin, sin], axis=-1))


def kernel(x, norm1_w, w_in, conv_w, a_log_fwd, dt_bias_fwd, a_log_bwd, dt_bias_bwd, gdn_norm_w,
           lambda_q1, lambda_k1, lambda_q2, lambda_k2, subln_w, w_proj_attn, w_proj_gdn, w_out,
           norm2_w, w_router, w_gate, w_up, w_down, norm_f_w):
    assert w_in.shape[0] == 1, "single-layer block"
    b, s, d = x.shape
    cap = CAP_FACTOR * s // N_EXPERTS
    w = w_in[0]
    small0 = 7 * D_MODEL
    small_w = 4 * N_HEADS
    w_main = jnp.concatenate([w[:, :small0], w[:, small0 + small_w:]], axis=1).astype(BF16)
    w_small = jnp.pad(w[:, small0:small0 + small_w], ((0, 0), (0, HEAD_W - small_w))).astype(BF16)

    p_main, p_small = _in_proj(x, norm1_w, w_main, w_small)

    cos, sin_signed = _rope_tables(s)
    oa = _attention(p_main, cos, sin_signed, lambda_q1, lambda_k1, lambda_q2, lambda_k2, subln_w)
    og = _gdn(p_main, p_small, conv_w[0], a_log_fwd[0], dt_bias_fwd[0], a_log_bwd[0],
              dt_bias_bwd[0], gdn_norm_w)

    w_r = jnp.pad(w_router[0], ((0, 0), (0, HEAD_W - N_EXPERTS)))
    x2, aff = _merge(x, oa, og, p_main, w_proj_attn[0].astype(BF16), w_proj_gdn[0].astype(BF16),
                     w_out[0].astype(BF16), norm2_w, w_r)
    idx, val = _route(aff, cap)
    return _moe(x2, idx, val, norm2_w, w_gate[0].astype(BF16), w_up[0].astype(BF16),
                w_down[0].astype(BF16), norm_f_w.reshape(1, d))
```

```python
import functools
import math

import jax
import jax.numpy as jnp
from jax import lax
from jax.experimental import pallas as pl
from jax.experimental.pallas import tpu as pltpu

F32 = jnp.float32
BF16 = jnp.bfloat16
HIGHEST = lax.Precision.HIGHEST

D_MODEL = 1024
N_HEADS = 8
HEAD_W = 128
DA_DH = 64
ROPE_THETA = 10000.0
CONV_K = 5
GDN_CHUNK = 64
N_EXPERTS = 16
EXPERT_FF = 2048
CAP_FACTOR = 2
NORM_EPS = 1e-6
L2_EPS = 1e-6
LAMBDA_INIT = 0.8 - 0.6 * math.exp(-0.3 * 0)

QA_BLK, KA_BLK, VA_BLK = 0, 8, 16
GQ_BLK, GK_BLK, GV_BLK, Z_BLK = 24, 32, 40, 48
GATE_A_BLK1024, GATE_G_BLK1024 = 7, 8
MAIN_W = 9 * D_MODEL

VMEM_LIMIT = 56 * 1024 * 1024


def _cparams(sem, vmem=VMEM_LIMIT):
    return pltpu.CompilerParams(dimension_semantics=sem, vmem_limit_bytes=vmem)


def _dot(a, b, precision=None):
    return jnp.dot(a, b, preferred_element_type=F32, precision=precision)


def _dot_nt(a, b, precision=None):
    return lax.dot_general(a, b, (((1,), (1,)), ((), ())), preferred_element_type=F32,
                           precision=precision)


def _dot_tn(a, b, precision=None):
    return lax.dot_general(a, b, (((0,), (0,)), ((), ())), preferred_element_type=F32,
                           precision=precision)


def _rms(t, w, eps=NORM_EPS):
    return t * lax.rsqrt(jnp.mean(t * t, axis=-1, keepdims=True) + eps) * w


def _inproj_kernel(x_ref, nw_ref, wm_ref, ws_ref, om_ref, os_ref, h_scr):
    @pl.when(pl.program_id(2) == 0)
    def _():
        h = _rms(x_ref[0], nw_ref[...]).astype(BF16)
        h_scr[...] = h
        os_ref[0] = _dot(h, ws_ref[...])

    om_ref[0] = _dot(h_scr[...], wm_ref[...])


def _in_proj(x, nw, w_main, w_small):
    b, s, d = x.shape
    tm = min(1024, s)
    tn = 1024
    return pl.pallas_call(
        _inproj_kernel,
        grid=(b, s // tm, MAIN_W // tn),
        in_specs=[
            pl.BlockSpec((1, tm, d), lambda i, m, n: (i, m, 0)),
            pl.BlockSpec((1, d), lambda i, m, n: (0, 0)),
            pl.BlockSpec((d, tn), lambda i, m, n: (0, n)),
            pl.BlockSpec((d, HEAD_W), lambda i, m, n: (0, 0)),
        ],
        out_specs=[
            pl.BlockSpec((1, tm, tn), lambda i, m, n: (i, m, n)),
            pl.BlockSpec((1, tm, HEAD_W), lambda i, m, n: (i, m, 0)),
        ],
        out_shape=[
            jax.ShapeDtypeStruct((b, s, MAIN_W), F32),
            jax.ShapeDtypeStruct((b, s, HEAD_W), F32),
        ],
        scratch_shapes=[pltpu.VMEM((tm, d), BF16)],
        compiler_params=_cparams(("parallel", "parallel", "arbitrary")),
        name="in_proj",
    )(x, nw, w_main, w_small)


def _attn_kernel(lq1_ref, lk1_ref, lq2_ref, lk2_ref, q_ref, k_ref, v_ref, cos_ref, sin_ref,
                 subw_ref, o_ref, kr_scr, vb_scr, *, tq):
    qi = pl.program_id(2)
    lane = lax.broadcasted_iota(jnp.int32, (1, HEAD_W), 1)
    first_half = (lane % DA_DH) < (DA_DH // 2)

    def rope(t, cos, sin_signed):
        rot = jnp.where(first_half, pltpu.roll(t, HEAD_W - DA_DH // 2, 1),
                        pltpu.roll(t, DA_DH // 2, 1))
        return t * cos + rot * sin_signed

    @pl.when(qi == 0)
    def _():
        kr_scr[...] = rope(k_ref[0], cos_ref[...], sin_ref[...]).astype(BF16)
        vb_scr[...] = v_ref[0].astype(BF16)

    lam = (jnp.exp(jnp.sum(lq1_ref[...] * lk1_ref[...], axis=-1, keepdims=True))
           - jnp.exp(jnp.sum(lq2_ref[...] * lk2_ref[...], axis=-1, keepdims=True))
           + LAMBDA_INIT)

    r0 = pl.multiple_of(qi * tq, tq)
    q = rope(q_ref[0], cos_ref[pl.ds(r0, tq), :], sin_ref[pl.ds(r0, tq), :]) * (DA_DH ** -0.5)
    lo = lane < DA_DH
    q2 = jnp.concatenate([jnp.where(lo, q, 0.0), jnp.where(lo, 0.0, q)], axis=0).astype(BF16)
    s = _dot_nt(q2, kr_scr[...])
    m = jnp.max(s, axis=-1, keepdims=True)
    p = jnp.exp(s - m)
    p = p * (1.0 / jnp.sum(p, axis=-1, keepdims=True))
    pd = p[:tq] - lam * p[tq:]
    o = _dot(pd.astype(BF16), vb_scr[...])
    o = _rms(o, subw_ref[...]) * (1.0 - LAMBDA_INIT)
    o_ref[0] = o.astype(o_ref.dtype)


def _attention(p_main, cos, sin_signed, lq1, lk1, lq2, lk2, subw):
    b, s, _ = p_main.shape
    tq = min(256, s)
    vec = pl.BlockSpec((1, DA_DH), lambda i, h, q: (0, 0))
    return pl.pallas_call(
        functools.partial(_attn_kernel, tq=tq),
        grid=(b, N_HEADS, s // tq),
        in_specs=[
            vec, vec, vec, vec,
            pl.BlockSpec((1, tq, HEAD_W), lambda i, h, q: (i, q, QA_BLK + h)),
            pl.BlockSpec((1, s, HEAD_W), lambda i, h, q: (i, 0, KA_BLK + h)),
            pl.BlockSpec((1, s, HEAD_W), lambda i, h, q: (i, 0, VA_BLK + h)),
            pl.BlockSpec((s, HEAD_W), lambda i, h, q: (0, 0)),
            pl.BlockSpec((s, HEAD_W), lambda i, h, q: (0, 0)),
            pl.BlockSpec((1, HEAD_W), lambda i, h, q: (0, 0)),
        ],
        out_specs=pl.BlockSpec((1, tq, HEAD_W), lambda i, h, q: (i, q, h)),
        out_shape=jax.ShapeDtypeStruct((b, s, N_HEADS * HEAD_W), BF16),
        scratch_shapes=[pltpu.VMEM((s, HEAD_W), BF16), pltpu.VMEM((s, HEAD_W), BF16)],
        compiler_params=_cparams(("parallel", "parallel", "arbitrary")),
        name="diff_attn",
    )(lq1, lk1, lq2, lk2, p_main, p_main, p_main, cos, sin_signed, subw)


def _softplus(x):
    return jnp.maximum(x, 0.0) + jnp.log1p(jnp.exp(-jnp.abs(x)))


def _gdn_kernel(alf_ref, dtf_ref, alb_ref, dtb_ref,
                q_ref, k_ref, v_ref, z_ref, sm_ref, cwq_ref, cwk_ref, cwv_ref, nw_ref,
                o_ref,
                pad_scr, q_scr, k_scr, v_scr, u_scr, w_scr, qd_scr, kd_scr, qk_scr, gl_scr,
                o_scr, *, seq):
    C = GDN_CHUNK
    n_chunks = seq // C
    h = pl.program_id(1)
    halo = 8
    conv_rows = min(512, seq)

    zeros_halo = jnp.zeros((halo, HEAD_W), F32)
    pad_scr[0:halo, :] = zeros_halo
    pad_scr[halo + seq:2 * halo + seq, :] = zeros_halo

    def conv_silu(src_ref, cw_ref, dst_scr, normalize):
        pad_scr[halo:halo + seq, :] = src_ref[0]
        cw = cw_ref[...]
        for r0 in range(0, seq, conv_rows):
            acc = None
            for t in range(CONV_K):
                start = halo + r0 + t - CONV_K // 2
                term = cw[t:t + 1, :] * pad_scr[start:start + conv_rows, :]
                acc = term if acc is None else acc + term
            y = acc * jax.nn.sigmoid(acc)
            if normalize:
                y = y * lax.rsqrt(jnp.sum(y * y, axis=-1, keepdims=True) + L2_EPS)
            dst_scr[r0:r0 + conv_rows, :] = y

    conv_silu(q_ref, cwq_ref, q_scr, True)
    conv_silu(k_ref, cwk_ref, k_scr, True)
    conv_silu(v_ref, cwv_ref, v_scr, False)

    ri = lax.broadcasted_iota(jnp.int32, (C, C), 0)
    ci = lax.broadcasted_iota(jnp.int32, (C, C), 1)
    incl = (ci <= ri, ci >= ri)
    strict = (ci < ri, ci > ri)
    lane = lax.broadcasted_iota(jnp.int32, (1, HEAD_W), 1)
    neg_decay_rate = (-jnp.exp(jnp.full((1, 1), alf_ref[h], F32)),
                      -jnp.exp(jnp.full((1, 1), alb_ref[h], F32)))
    dt_bias = (dtf_ref[h], dtb_ref[h])
    dk_scale = HEAD_W ** -0.5

    def lane_pick(t, idx):
        return jnp.sum(jnp.where(lane == idx, t, 0.0), axis=-1, keepdims=True)

    def split_bf16(t, terms):
        parts = []
        for _ in range(terms - 1):
            p = t.astype(BF16)
            parts.append(p)
            t = t - p.astype(F32)
        parts.append(t.astype(BF16))
        return parts

    def bdot(a, b):
        return lax.dot_general(a, b, (((2,), (1,)), ((0,), (0,))), preferred_element_type=F32)

    def bdot_nt(a, b):
        return lax.dot_general(a, b, (((2,), (2,)), ((0,), (0,))), preferred_element_type=F32)

    def unit_tri_inverse_minus_eye(a):
        a_hi, a_lo = split_bf16(a, 2)
        pw = bdot(a_hi, a_hi) + (bdot(a_hi, a_lo) + bdot(a_lo, a_hi))
        r = -a
        for step in range(int(math.log2(C)) - 1):
            if step > 0:
                pw16 = pw.astype(BF16)
                pw = bdot(pw16, pw16)
            r = r + pw + bdot(r.astype(BF16), pw.astype(BF16))
        return r

    nb = min(8, n_chunks)
    rows_b = nb * C
    incl16 = tuple(jnp.broadcast_to(m.astype(BF16), (nb, C, C)) for m in incl)
    first_lane16 = jnp.broadcast_to(
        (lax.broadcasted_iota(jnp.int32, (C, HEAD_W), 1) == 0).astype(BF16), (nb, C, HEAD_W))

    def precompute(blk, carry):
        rows = pl.ds(pl.multiple_of(blk * rows_b, rows_b), rows_b)
        q = (q_scr[rows, :] * dk_scale).reshape(nb, C, HEAD_W)
        k = k_scr[rows, :].reshape(nb, C, HEAD_W)
        v = v_scr[rows, :].reshape(nb, C, HEAD_W)
        sm = sm_ref[0, rows, :].reshape(nb, C, HEAD_W)
        k16 = k.astype(BF16)
        qk_raw = bdot_nt(q.astype(BF16), k16)
        for d in range(2):
            a_logit = lane_pick(sm, d * N_HEADS + h)
            b_logit = lane_pick(sm, (2 + d) * N_HEADS + h)
            g = neg_decay_rate[d] * _softplus(a_logit + dt_bias[d])
            beta = jax.nn.sigmoid(b_logit)
            g_parts = split_bf16(jnp.broadcast_to(g, (nb, C, HEAD_W)), 3)
            cum = (bdot(incl16[d], g_parts[2]) + bdot(incl16[d], g_parts[1])) + bdot(incl16[d], g_parts[0])
            cum_i = cum[:, :, :C]
            c_parts = split_bf16(cum, 3)
            cum_j = ((bdot_nt(first_lane16, c_parts[2]) + bdot_nt(first_lane16, c_parts[1]))
                     + bdot_nt(first_lane16, c_parts[0]))
            decay = jnp.where(incl[d], jnp.exp(jnp.where(incl[d], cum_i - cum_j, 0.0)), 0.0)
            k_beta = k * beta
            a_mat = jnp.where(strict[d], bdot_nt(k_beta.astype(BF16), k16) * decay, 0.0)
            r16 = unit_tri_inverse_minus_eye(a_mat).astype(BF16)
            e_cum = jnp.exp(cum)
            v_beta = v * beta
            kb_dec = k_beta * e_cum
            u = v_beta + bdot(r16, v_beta.astype(BF16))
            w = kb_dec + bdot(r16, kb_dec.astype(BF16))
            cum_end = cum[:, C - 1:C, :] if d == 0 else cum[:, 0:1, :]
            u_scr[d, rows, :] = u.reshape(rows_b, HEAD_W)
            w_scr[d, rows, :] = w.astype(BF16).reshape(rows_b, HEAD_W)
            qd_scr[d, rows, :] = (q * e_cum).astype(BF16).reshape(rows_b, HEAD_W)
            kd_scr[d, rows, :] = (k * jnp.exp(cum_end - cum)).astype(BF16).reshape(rows_b, HEAD_W)
            qk_scr[d, rows, :] = jnp.where(incl[d], qk_raw * decay, 0.0).astype(BF16).reshape(rows_b, C)
            gl_scr[d, pl.ds(pl.multiple_of(blk * (nb * 8), nb * 8), nb * 8), :] = jnp.broadcast_to(
                jnp.exp(cum_end), (nb, 8, HEAD_W)).reshape(nb * 8, HEAD_W)
        return carry

    lax.fori_loop(0, n_chunks // nb, precompute, 0)

    def scan_step(t, states):
        new_states = []
        for d in range(2):
            n = t if d == 0 else n_chunks - 1 - t
            rows = pl.ds(pl.multiple_of(n * C, C), C)
            st = states[d]
            st16 = st.astype(BF16)
            v_new = u_scr[d, rows, :] - _dot(w_scr[d, rows, :], st16)
            vn16 = v_new.astype(BF16)
            o = _dot(qd_scr[d, rows, :], st16) + _dot(qk_scr[d, rows, :], vn16)
            gl = gl_scr[d, pl.ds(pl.multiple_of(n * 8, 8), 1), :]
            new_states.append(st * gl + _dot_tn(kd_scr[d, rows, :], vn16))
            o_scr[d, rows, :] = o
        return tuple(new_states)

    zero_state = jnp.zeros((HEAD_W, HEAD_W), F32)
    lax.fori_loop(0, n_chunks, scan_step, (zero_state, zero_state))

    for r0 in range(0, seq, conv_rows):
        sl = slice(r0, r0 + conv_rows)
        o = _rms(o_scr[0, sl, :] + o_scr[1, sl, :], nw_ref[...])
        z = z_ref[0, sl, :]
        o_ref[0, sl, :] = (o * (z * jax.nn.sigmoid(z))).astype(o_ref.dtype)


def _gdn(p_main, p_small, conv_w, alf, dtf, alb, dtb, nw):
    b, s, _ = p_main.shape
    smem = pl.BlockSpec(memory_space=pltpu.SMEM)

    def col(blk):
        return pl.BlockSpec((1, s, HEAD_W), lambda i, h: (i, 0, blk + h))

    def cw(blk):
        return pl.BlockSpec((CONV_K, HEAD_W), lambda i, h: (0, blk + h))

    return pl.pallas_call(
        functools.partial(_gdn_kernel, seq=s),
        grid=(b, N_HEADS),
        in_specs=[
            smem, smem, smem, smem,
            col(GQ_BLK), col(GK_BLK), col(GV_BLK), col(Z_BLK),
            pl.BlockSpec((1, s, HEAD_W), lambda i, h: (i, 0, 0)),
            cw(0), cw(N_HEADS), cw(2 * N_HEADS),
            pl.BlockSpec((1, HEAD_W), lambda i, h: (0, 0)),
        ],
        out_specs=pl.BlockSpec((1, s, HEAD_W), lambda i, h: (i, 0, h)),
        out_shape=jax.ShapeDtypeStruct((b, s, N_HEADS * HEAD_W), BF16),
        scratch_shapes=[
            pltpu.VMEM((s + 16, HEAD_W), F32),
            pltpu.VMEM((s, HEAD_W), F32),
            pltpu.VMEM((s, HEAD_W), F32),
            pltpu.VMEM((s, HEAD_W), F32),
            pltpu.VMEM((2, s, HEAD_W), F32),
            pltpu.VMEM((2, s, HEAD_W), BF16),
            pltpu.VMEM((2, s, HEAD_W), BF16),
            pltpu.VMEM((2, s, HEAD_W), BF16),
            pltpu.VMEM((2, s, GDN_CHUNK), BF16),
            pltpu.VMEM((2, (s // GDN_CHUNK) * 8, HEAD_W), F32),
            pltpu.VMEM((2, s, HEAD_W), F32),
        ],
        compiler_params=_cparams(("parallel", "arbitrary")),
        name="gdn",
    )(alf, dtf, alb, dtb, p_main, p_main, p_main, p_main, p_small, conv_w, conv_w, conv_w, nw)


def _merge_kernel(x_ref, oa_ref, og_ref, ga_ref, gg_ref, wpa_ref, wpg_ref, wo_ref, n2w_ref,
                  wr_ref, x2_ref, aff_ref):
    ya = _dot(oa_ref[0], wpa_ref[...])
    yg = _dot(og_ref[0], wpg_ref[...])
    merged = jax.nn.sigmoid(ga_ref[0]) * ya + jax.nn.sigmoid(gg_ref[0]) * yg
    x2 = x_ref[0] + _dot(merged.astype(BF16), wo_ref[...])
    x2_ref[0] = x2
    h2 = _rms(x2, n2w_ref[...])
    logits = _dot(h2, wr_ref[...], HIGHEST)
    lane = lax.broadcasted_iota(jnp.int32, logits.shape, 1)
    valid = lane < N_EXPERTS
    logits = jnp.where(valid, logits, -1e30)
    e = jnp.exp(logits - jnp.max(logits, axis=-1, keepdims=True))
    e = jnp.where(valid, e, 0.0)
    aff_ref[0] = e * (1.0 / jnp.sum(e, axis=-1, keepdims=True))


def _merge(x, oa, og, p_main, wpa, wpg, wo, n2w, wr):
    b, s, d = x.shape
    tm = min(512, s)
    tok = lambda i, m: (i, m, 0)
    const = lambda i, m: (0, 0)
    return pl.pallas_call(
        _merge_kernel,
        grid=(b, s // tm),
        in_specs=[
            pl.BlockSpec((1, tm, d), tok),
            pl.BlockSpec((1, tm, d), tok),
            pl.BlockSpec((1, tm, d), tok),
            pl.BlockSpec((1, tm, d), lambda i, m: (i, m, GATE_A_BLK1024)),
            pl.BlockSpec((1, tm, d), lambda i, m: (i, m, GATE_G_BLK1024)),
            pl.BlockSpec((d, d), const),
            pl.BlockSpec((d, d), const),
            pl.BlockSpec((d, d), const),
            pl.BlockSpec((1, d), const),
            pl.BlockSpec((d, HEAD_W), const),
        ],
        out_specs=[pl.BlockSpec((1, tm, d), tok), pl.BlockSpec((1, tm, HEAD_W), tok)],
        out_shape=[jax.ShapeDtypeStruct((b, s, d), F32),
                   jax.ShapeDtypeStruct((b, s, HEAD_W), F32)],
        compiler_params=_cparams(("parallel", "parallel")),
        name="merge",
    )(x, oa, og, p_main, p_main, wpa, wpg, wo, n2w, wr)


def _route_kernel(aff_ref, idx_ref, val_ref, csm_scr, *, seq, cap):
    blk = 128
    bits = lax.bitcast_convert_type(aff_ref[0], jnp.int32)

    def bisect(i, cur):
        cand = cur | jnp.left_shift(jnp.int32(1), 30 - i)
        cnt = jnp.sum((bits >= cand).astype(F32), axis=0, keepdims=True)
        return jnp.where(cnt >= cap, cand, cur)

    thr = lax.fori_loop(0, 31, bisect, jnp.zeros((1, HEAD_W), jnp.int32))
    n_gt = jnp.sum((bits > thr).astype(F32), axis=0, keepdims=True)
    n_ties = cap - n_gt

    ri = lax.broadcasted_iota(jnp.int32, (blk, blk), 0)
    ci = lax.broadcasted_iota(jnp.int32, (blk, blk), 1)
    ltri = (ci <= ri).astype(BF16)

    def select(j, carry):
        tie_base, sel_base = carry
        rows = pl.ds(pl.multiple_of(j * blk, blk), blk)
        bj = lax.bitcast_convert_type(aff_ref[0, rows, :], jnp.int32)
        eq = jnp.where(bj == thr, 1.0, 0.0)
        tie_rank = _dot(ltri, eq.astype(BF16)) + tie_base - eq
        sel = jnp.where(bj > thr, 1.0, eq * jnp.where(tie_rank < n_ties, 1.0, 0.0))
        cs = _dot(ltri, sel.astype(BF16)) + sel_base
        csm_scr[rows, :] = cs * sel
        return (tie_base + jnp.sum(eq, axis=0, keepdims=True),
                sel_base + jnp.sum(sel, axis=0, keepdims=True))

    zero = jnp.zeros((1, HEAD_W), F32)
    lax.fori_loop(0, seq // blk, select, (zero, zero))

    rb = min(512, seq)
    slot = lax.broadcasted_iota(jnp.int32, (1, cap), 1).astype(F32) + 1.0
    for e in range(N_EXPERTS):
        def gather_slots(j, carry):
            idx_acc, val_acc = carry
            r0 = pl.multiple_of(j * rb, rb)
            rows = pl.ds(r0, rb)
            onehot = csm_scr[rows, e:e + 1] == slot
            tok = (lax.broadcasted_iota(jnp.int32, (rb, 1), 0) + r0).astype(F32)
            idx_acc = idx_acc + jnp.sum(jnp.where(onehot, tok, 0.0), axis=0, keepdims=True)
            val_acc = val_acc + jnp.sum(jnp.where(onehot, aff_ref[0, rows, e:e + 1], 0.0),
                                        axis=0, keepdims=True)
            return idx_acc, val_acc

        zc = jnp.zeros((1, cap), F32)
        idx_e, val_e = lax.fori_loop(0, seq // rb, gather_slots, (zc, zc))
        idx_ref[0, e:e + 1, :] = idx_e.astype(jnp.int32)
        val_ref[0, e:e + 1, :] = val_e


def _route(aff, cap):
    b, s, _ = aff.shape
    return pl.pallas_call(
        functools.partial(_route_kernel, seq=s, cap=cap),
        grid=(b,),
        in_specs=[pl.BlockSpec((1, s, HEAD_W), lambda i: (i, 0, 0))],
        out_specs=[pl.BlockSpec((1, N_EXPERTS, cap), lambda i: (i, 0, 0)),
                   pl.BlockSpec((1, N_EXPERTS, cap), lambda i: (i, 0, 0))],
        out_shape=[jax.ShapeDtypeStruct((b, N_EXPERTS, cap), jnp.int32),
                   jax.ShapeDtypeStruct((b, N_EXPERTS, cap), F32)],
        scratch_shapes=[pltpu.VMEM((s, HEAD_W), F32)],
        compiler_params=_cparams(("parallel",)),
        name="route",
    )(aff)


def _moe_kernel(idx_ref, val_ref, x2_hbm, n2w_ref, wg_ref, wu_ref, wd_ref, nfw_ref, out_hbm,
                x2_v, acc_v, xg_v, xs_v, y_v, in_sem, out_sem, *, seq, cap, n_ff):
    b = pl.program_id(0)
    e = pl.program_id(1)
    f = pl.program_id(2)
    slot_row = b * N_EXPERTS + e

    @pl.when((e == 0) & (f == 0))
    def _():
        cp = pltpu.make_async_copy(x2_hbm.at[b], x2_v, in_sem)
        cp.start()
        acc_v[...] = jnp.zeros_like(acc_v)
        cp.wait()

    @pl.when(f == 0)
    def _():
        def gather(c, carry):
            tok = idx_ref[slot_row, c]
            xg_v[pl.ds(c, 1), :] = x2_v[pl.ds(tok, 1), :]
            return carry

        lax.fori_loop(0, cap, gather, 0, unroll=8)
        xs_v[...] = _rms(xg_v[...], n2w_ref[...]).astype(BF16)

    xs = xs_v[...]
    gate = _dot(xs, wg_ref[0])
    up = _dot(xs, wu_ref[0])
    act = (gate * jax.nn.sigmoid(gate) * up).astype(BF16)
    y_part = _dot(act, wd_ref[0])

    @pl.when(f == 0)
    def _():
        y_v[...] = y_part

    @pl.when(f > 0)
    def _():
        y_v[...] += y_part

    @pl.when(f == n_ff - 1)
    def _():
        def scatter(c, carry):
            tok = idx_ref[slot_row, c]
            wgt = val_ref[slot_row, c]
            acc_v[pl.ds(tok, 1), :] += y_v[pl.ds(c, 1), :] * wgt
            return carry

        lax.fori_loop(0, cap, scatter, 0, unroll=8)

    @pl.when((e == N_EXPERTS - 1) & (f == n_ff - 1))
    def _():
        rb = min(512, seq)
        for r0 in range(0, seq, rb):
            sl = slice(r0, r0 + rb)
            acc_v[sl, :] = _rms(x2_v[sl, :] + acc_v[sl, :], nfw_ref[...])
        cp = pltpu.make_async_copy(acc_v, out_hbm.at[b], out_sem)
        cp.start()
        cp.wait()


def _moe(x2, idx, val, n2w, wg, wu, wd, nfw):
    b, s, d = x2.shape
    cap = idx.shape[-1]
    tf = 1024
    n_ff = EXPERT_FF // tf
    grid_spec = pltpu.PrefetchScalarGridSpec(
        num_scalar_prefetch=2,
        grid=(b, N_EXPERTS, n_ff),
        in_specs=[
            pl.BlockSpec(memory_space=pl.ANY),
            pl.BlockSpec((1, d), lambda i, e, f, *_: (0, 0)),
            pl.BlockSpec((1, d, tf), lambda i, e, f, *_: (e, 0, f)),
            pl.BlockSpec((1, d, tf), lambda i, e, f, *_: (e, 0, f)),
            pl.BlockSpec((1, tf, d), lambda i, e, f, *_: (e, f, 0)),
            pl.BlockSpec((1, d), lambda i, e, f, *_: (0, 0)),
        ],
        out_specs=pl.BlockSpec(memory_space=pl.ANY),
        scratch_shapes=[
            pltpu.VMEM((s, d), F32),
            pltpu.VMEM((s, d), F32),
            pltpu.VMEM((cap, d), F32),
            pltpu.VMEM((cap, d), BF16),
            pltpu.VMEM((cap, d), F32),
            pltpu.SemaphoreType.DMA,
            pltpu.SemaphoreType.DMA,
        ],
    )
    return pl.pallas_call(
        functools.partial(_moe_kernel, seq=s, cap=cap, n_ff=n_ff),
        grid_spec=grid_spec,
        out_shape=jax.ShapeDtypeStruct((b, s, d), F32),
        compiler_params=_cparams(("arbitrary", "arbitrary", "arbitrary"), vmem=60 * 1024 * 1024),
        name="moe",
    )(idx.reshape(b * N_EXPERTS, cap), val.reshape(b * N_EXPERTS, cap), x2, n2w, wg, wu, wd, nfw)


def _rope_tables(seq):
    inv = ROPE_THETA ** (-jnp.arange(0, DA_DH, 2, dtype=F32) / DA_DH)
    ang = jnp.arange(seq, dtype=F32)[:, None] * inv[None, :]
    cos, sin = jnp.cos(ang), jnp.sin(ang)
    return (jnp.concatenate([cos, cos, cos, cos], axis=-1),
            jnp.concatenate([-sin, sin, -sin, sin], axis=-1))


def kernel(x, norm1_w, w_in, conv_w, a_log_fwd, dt_bias_fwd, a_log_bwd, dt_bias_bwd, gdn_norm_w,
           lambda_q1, lambda_k1, lambda_q2, lambda_k2, subln_w, w_proj_attn, w_proj_gdn, w_out,
           norm2_w, w_router, w_gate, w_up, w_down, norm_f_w):
    assert w_in.shape[0] == 1, "single-layer block"
    b, s, d = x.shape
    cap = CAP_FACTOR * s // N_EXPERTS
    w = w_in[0]
    small0 = 7 * D_MODEL
    small_w = 4 * N_HEADS
    w_main = jnp.concatenate([w[:, :small0], w[:, small0 + small_w:]], axis=1).astype(BF16)
    w_small = jnp.pad(w[:, small0:small0 + small_w], ((0, 0), (0, HEAD_W - small_w))).astype(BF16)

    p_main, p_small = _in_proj(x, norm1_w, w_main, w_small)

    cos, sin_signed = _rope_tables(s)
    oa = _attention(p_main, cos, sin_signed, lambda_q1, lambda_k1, lambda_q2, lambda_k2, subln_w)
    og = _gdn(p_main, p_small, conv_w[0], a_log_fwd[0], dt_bias_fwd[0], a_log_bwd[0],
              dt_bias_bwd[0], gdn_norm_w)

    w_r = jnp.pad(w_router[0], ((0, 0), (0, HEAD_W - N_EXPERTS)))
    x2, aff = _merge(x, oa, og, p_main, w_proj_attn[0].astype(BF16), w_proj_gdn[0].astype(BF16),
                     w_out[0].astype(BF16), norm2_w, w_r)
    idx, val = _route(aff, cap)
    return _moe(x2, idx, val, norm2_w, w_gate[0].astype(BF16), w_up[0].astype(BF16),
                w_down[0].astype(BF16), norm_f_w.reshape(1, d))
```

```python
import functools
import math

import jax
import jax.numpy as jnp
from jax import lax
from jax.experimental import pallas as pl
from jax.experimental.pallas import tpu as pltpu

F32 = jnp.float32
BF16 = jnp.bfloat16
HIGHEST = lax.Precision.HIGHEST

D_MODEL = 1024
N_HEADS = 8
HEAD_W = 128
LANE_GROUPS = D_MODEL // HEAD_W
DA_DH = 64
ROPE_THETA = 10000.0
CONV_K = 5
GDN_CHUNK = 64
ATTN_KEY_CHUNK = 512
ONES_ROWS = 16
SCATTER_BATCH = 8
N_EXPERTS = 16
EXPERT_FF = 2048
CAP_FACTOR = 2
NORM_EPS = 1e-6
L2_EPS = 1e-6
LAMBDA_INIT = 0.8 - 0.6 * math.exp(-0.3 * 0)

QA_BLK, KA_BLK, VA_BLK = 0, 8, 16
GQ_BLK, GK_BLK, GV_BLK, Z_BLK = 24, 32, 40, 48
GATE_A_BLK1024, GATE_G_BLK1024 = 7, 8
MAIN_W = 9 * D_MODEL

VMEM_LIMIT = 56 * 1024 * 1024


def _cparams(sem, vmem=VMEM_LIMIT):
    return pltpu.CompilerParams(dimension_semantics=sem, vmem_limit_bytes=vmem)


def _dot(a, b, precision=None):
    return jnp.dot(a, b, preferred_element_type=F32, precision=precision)


def _dot_nt(a, b, precision=None):
    return lax.dot_general(a, b, (((1,), (1,)), ((), ())), preferred_element_type=F32,
                           precision=precision)


def _dot_tn(a, b, precision=None):
    return lax.dot_general(a, b, (((0,), (0,)), ((), ())), preferred_element_type=F32,
                           precision=precision)


def _rms(t, w, eps=NORM_EPS):
    return t * lax.rsqrt(jnp.mean(t * t, axis=-1, keepdims=True) + eps) * w


def _inproj_kernel(x_ref, nw_ref, wm_ref, ws_ref, om_ref, os_ref, h_scr):
    @pl.when(pl.program_id(2) == 0)
    def _():
        h = _rms(x_ref[0], nw_ref[...]).astype(BF16)
        h_scr[...] = h
        os_ref[0] = _dot(h, ws_ref[...])

    om_ref[0] = _dot(h_scr[...], wm_ref[...])


def _in_proj(x, nw, w_main, w_small):
    b, s, d = x.shape
    tm = min(1024, s)
    tn = 1024
    return pl.pallas_call(
        _inproj_kernel,
        grid=(b, s // tm, MAIN_W // tn),
        in_specs=[
            pl.BlockSpec((1, tm, d), lambda i, m, n: (i, m, 0)),
            pl.BlockSpec((1, d), lambda i, m, n: (0, 0)),
            pl.BlockSpec((d, tn), lambda i, m, n: (0, n)),
            pl.BlockSpec((d, HEAD_W), lambda i, m, n: (0, 0)),
        ],
        out_specs=[
            pl.BlockSpec((1, tm, tn), lambda i, m, n: (i, m, n)),
            pl.BlockSpec((1, tm, HEAD_W), lambda i, m, n: (i, m, 0)),
        ],
        out_shape=[
            jax.ShapeDtypeStruct((b, s, MAIN_W), F32),
            jax.ShapeDtypeStruct((b, s, HEAD_W), F32),
        ],
        scratch_shapes=[pltpu.VMEM((tm, d), BF16)],
        compiler_params=_cparams(("parallel", "parallel", "arbitrary")),
        name="in_proj",
    )(x, nw, w_main, w_small)


def _attn_kernel(lq1_ref, lk1_ref, lq2_ref, lk2_ref, q_ref, k_ref, v_ref, cos_ref, sin_ref,
                 subw_ref, o_ref, kr_scr, vt_scr, m_scr, *chunk_scr, tq, seq):
    qi = pl.program_id(2)
    n_q = seq // tq
    kc = min(ATTN_KEY_CHUNK, seq)
    n_kc = seq // kc
    slab = 64
    s_scr = (chunk_scr[:n_kc], chunk_scr[n_kc:2 * n_kc])
    p_scr = chunk_scr[2 * n_kc:]
    lane = lax.broadcasted_iota(jnp.int32, (1, HEAD_W), 1)
    first_half = (lane % DA_DH) < (DA_DH // 2)
    lo = lane < DA_DH

    def rope(t, cos, sin_signed):
        rot = jnp.where(first_half, pltpu.roll(t, HEAD_W - DA_DH // 2, 1),
                        pltpu.roll(t, DA_DH // 2, 1))
        return t * cos + rot * sin_signed

    def stacked_queries(tile):
        rows = pl.ds(pl.multiple_of(tile * tq, tq), tq)
        q = rope(q_ref[0, rows, :], cos_ref[rows, :], sin_ref[rows, :]) * (DA_DH ** -0.5)
        return jnp.concatenate([jnp.where(lo, q, 0.0), jnp.where(lo, 0.0, q)], axis=0).astype(BF16)

    def scores_and_max(q2, dst, c, m8):
        dst[...] = _dot_nt(kr_scr[c * kc:(c + 1) * kc, :], q2)
        for r0s in range(0, kc, slab):
            st = dst[r0s:r0s + slab, :]
            m8 = jnp.maximum(m8, jnp.max(st.reshape(slab // 8, 8, 2 * tq), axis=0))
        return m8

    neg = jnp.full((8, 2 * tq), -1e30, F32)

    @pl.when(qi == 0)
    def _():
        kr_scr[...] = rope(k_ref[0], cos_ref[...], sin_ref[...]).astype(BF16)
        for c in range(n_kc):
            vt_scr[c, :HEAD_W, :] = v_ref[0, c * kc:(c + 1) * kc, :].T.astype(BF16)
            vt_scr[c, HEAD_W:, :] = jnp.ones((ONES_ROWS, kc), BF16)
        q2 = stacked_queries(0)
        m8 = neg
        for c in range(n_kc):
            m8 = scores_and_max(q2, s_scr[0][c], c, m8)
        m_scr[...] = m8

    lam = (jnp.exp(jnp.sum(lq1_ref[...] * lk1_ref[...], axis=-1, keepdims=True))
           - jnp.exp(jnp.sum(lq2_ref[...] * lk2_ref[...], axis=-1, keepdims=True))
           + LAMBDA_INIT)

    def tile(parity):
        q2_next = stacked_queries((qi + 1) % n_q)
        m = jnp.max(m_scr[...], axis=0, keepdims=True)
        m8 = neg
        acc = jnp.zeros((HEAD_W + ONES_ROWS, 2 * tq), F32)
        for c in range(n_kc):
            m8 = scores_and_max(q2_next, s_scr[1 - parity][c], c, m8)
            for r0s in range(0, kc, slab):
                p = jnp.exp(s_scr[parity][c][r0s:r0s + slab, :] - m)
                p_scr[c][r0s:r0s + slab, :] = p.astype(BF16)
            acc = acc + _dot(vt_scr[c], p_scr[c][...])
        m_scr[...] = m8
        inv_l = 1.0 / acc[HEAD_W:HEAD_W + 1, :]
        acc = acc[:HEAD_W, :]
        o_t = acc[:, :tq] * inv_l[:, :tq] - lam * (acc[:, tq:] * inv_l[:, tq:])
        o = _rms(o_t.T, subw_ref[...]) * (1.0 - LAMBDA_INIT)
        o_ref[0] = o.astype(o_ref.dtype)

    for parity in range(2):
        pl.when(qi % 2 == parity)(functools.partial(tile, parity))


def _attention(p_main, cos, sin_signed, lq1, lk1, lq2, lk2, subw):
    b, s, _ = p_main.shape
    tq = min(256, s // 2)
    kc = min(ATTN_KEY_CHUNK, s)
    n_kc = s // kc
    vec = pl.BlockSpec((1, DA_DH), lambda i, h, q: (0, 0))

    def col(blk):
        return pl.BlockSpec((1, s, HEAD_W), lambda i, h, q: (i, 0, blk + h))

    return pl.pallas_call(
        functools.partial(_attn_kernel, tq=tq, seq=s),
        grid=(b, N_HEADS, s // tq),
        in_specs=[
            vec, vec, vec, vec,
            col(QA_BLK), col(KA_BLK), col(VA_BLK),
            pl.BlockSpec((s, HEAD_W), lambda i, h, q: (0, 0)),
            pl.BlockSpec((s, HEAD_W), lambda i, h, q: (0, 0)),
            pl.BlockSpec((1, HEAD_W), lambda i, h, q: (0, 0)),
        ],
        out_specs=pl.BlockSpec((1, tq, HEAD_W), lambda i, h, q: (i, q, h)),
        out_shape=jax.ShapeDtypeStruct((b, s, N_HEADS * HEAD_W), BF16),
        scratch_shapes=([pltpu.VMEM((s, HEAD_W), BF16),
                         pltpu.VMEM((n_kc, HEAD_W + ONES_ROWS, kc), BF16),
                         pltpu.VMEM((8, 2 * tq), F32)]
                        + [pltpu.VMEM((kc, 2 * tq), F32)] * (2 * n_kc)
                        + [pltpu.VMEM((kc, 2 * tq), BF16)] * n_kc),
        compiler_params=_cparams(("parallel", "parallel", "arbitrary")),
        name="diff_attn",
    )(lq1, lk1, lq2, lk2, p_main, p_main, p_main, cos, sin_signed, subw)


def _softplus(x):
    return jnp.maximum(x, 0.0) + jnp.log1p(jnp.exp(-jnp.abs(x)))


def _gdn_kernel(alf_ref, dtf_ref, alb_ref, dtb_ref,
                q_ref, k_ref, v_ref, z_ref, sm_ref, cwq_ref, cwk_ref, cwv_ref, nw_ref,
                o_ref,
                pad_scr, q_scr, k_scr, v_scr, sw_scr, su_scr, os_scr, ou_scr, gl_scr,
                o_scr, *, seq):
    C = GDN_CHUNK
    n_chunks = seq // C
    h = pl.program_id(1)
    halo = 8
    conv_rows = min(512, seq)

    zeros_halo = jnp.zeros((halo, HEAD_W), F32)
    pad_scr[0:halo, :] = zeros_halo
    pad_scr[halo + seq:2 * halo + seq, :] = zeros_halo

    def conv_silu(src_ref, cw_ref, dst_scr, normalize):
        pad_scr[halo:halo + seq, :] = src_ref[0]
        cw = cw_ref[...]
        for r0 in range(0, seq, conv_rows):
            acc = None
            for t in range(CONV_K):
                start = halo + r0 + t - CONV_K // 2
                term = cw[t:t + 1, :] * pad_scr[start:start + conv_rows, :]
                acc = term if acc is None else acc + term
            y = acc * jax.nn.sigmoid(acc)
            if normalize:
                y = y * lax.rsqrt(jnp.sum(y * y, axis=-1, keepdims=True) + L2_EPS)
            dst_scr[r0:r0 + conv_rows, :] = y

    conv_silu(q_ref, cwq_ref, q_scr, True)
    conv_silu(k_ref, cwk_ref, k_scr, True)
    conv_silu(v_ref, cwv_ref, v_scr, False)

    ri = lax.broadcasted_iota(jnp.int32, (C, C), 0)
    ci = lax.broadcasted_iota(jnp.int32, (C, C), 1)
    incl = (ci <= ri, ci >= ri)
    strict = (ci < ri, ci > ri)
    lane = lax.broadcasted_iota(jnp.int32, (1, HEAD_W), 1)
    neg_decay_rate = (-jnp.exp(jnp.full((1, 1), alf_ref[h], F32)),
                      -jnp.exp(jnp.full((1, 1), alb_ref[h], F32)))
    dt_bias = (dtf_ref[h], dtb_ref[h])
    dk_scale = HEAD_W ** -0.5

    def lane_pick(t, idx):
        return jnp.sum(jnp.where(lane == idx, t, 0.0), axis=-1, keepdims=True)

    def split_bf16(t, terms):
        parts = []
        for _ in range(terms - 1):
            p = t.astype(BF16)
            parts.append(p)
            t = t - p.astype(F32)
        parts.append(t.astype(BF16))
        return parts

    def bdot(a, b):
        return lax.dot_general(a, b, (((2,), (1,)), ((0,), (0,))), preferred_element_type=F32)

    def bdot_nt(a, b):
        return lax.dot_general(a, b, (((2,), (2,)), ((0,), (0,))), preferred_element_type=F32)

    def bdot_tn(a, b):
        return lax.dot_general(a, b, (((1,), (1,)), ((0,), (0,))), preferred_element_type=F32)

    def unit_tri_inverse_minus_eye(a):
        a_hi, a_lo = split_bf16(a, 2)
        pw = bdot(a_hi, a_hi) + (bdot(a_hi, a_lo) + bdot(a_lo, a_hi))
        r = -a
        for step in range(int(math.log2(C)) - 1):
            if step > 0:
                pw16 = pw.astype(BF16)
                pw = bdot(pw16, pw16)
            r = r + pw + bdot(r.astype(BF16), pw.astype(BF16))
        return r

    nb = min(8, n_chunks)
    rows_b = nb * C
    incl16 = tuple(jnp.broadcast_to(m.astype(BF16), (nb, C, C)) for m in incl)
    first_lane16 = jnp.broadcast_to(
        (lax.broadcasted_iota(jnp.int32, (C, HEAD_W), 1) == 0).astype(BF16), (nb, C, HEAD_W))

    def precompute(blk, carry):
        rows = pl.ds(pl.multiple_of(blk * rows_b, rows_b), rows_b)
        q = (q_scr[rows, :] * dk_scale).reshape(nb, C, HEAD_W)
        k = k_scr[rows, :].reshape(nb, C, HEAD_W)
        v = v_scr[rows, :].reshape(nb, C, HEAD_W)
        sm = sm_ref[0, rows, :].reshape(nb, C, HEAD_W)
        k16 = k.astype(BF16)
        qk_raw = bdot_nt(q.astype(BF16), k16)
        for d in range(2):
            a_logit = lane_pick(sm, d * N_HEADS + h)
            b_logit = lane_pick(sm, (2 + d) * N_HEADS + h)
            g = neg_decay_rate[d] * _softplus(a_logit + dt_bias[d])
            beta = jax.nn.sigmoid(b_logit)
            g_parts = split_bf16(jnp.broadcast_to(g, (nb, C, HEAD_W)), 3)
            cum = (bdot(incl16[d], g_parts[2]) + bdot(incl16[d], g_parts[1])) + bdot(incl16[d], g_parts[0])
            cum_i = cum[:, :, :C]
            c_parts = split_bf16(cum, 3)
            cum_j = ((bdot_nt(first_lane16, c_parts[2]) + bdot_nt(first_lane16, c_parts[1]))
                     + bdot_nt(first_lane16, c_parts[0]))
            decay = jnp.where(incl[d], jnp.exp(jnp.where(incl[d], cum_i - cum_j, 0.0)), 0.0)
            k_beta = k * beta
            a_mat = jnp.where(strict[d], bdot_nt(k_beta.astype(BF16), k16) * decay, 0.0)
            r16 = unit_tri_inverse_minus_eye(a_mat).astype(BF16)
            e_cum = jnp.exp(cum)
            v_beta = v * beta
            kb_dec = k_beta * e_cum
            u = v_beta + bdot(r16, v_beta.astype(BF16))
            w = kb_dec + bdot(r16, kb_dec.astype(BF16))
            cum_end = cum[:, C - 1:C, :] if d == 0 else cum[:, 0:1, :]
            wu16 = jnp.concatenate([w.astype(BF16), u.astype(BF16)], axis=-1)
            kd16 = (k * jnp.exp(cum_end - cum)).astype(BF16)
            qk16 = jnp.where(incl[d], qk_raw * decay, 0.0).astype(BF16)
            kd_wu = bdot_tn(kd16, wu16)
            qk_wu = bdot(qk16, wu16)
            rows_s = pl.ds(pl.multiple_of(blk * (nb * HEAD_W), nb * HEAD_W), nb * HEAD_W)
            sw_scr[d, rows_s, :] = kd_wu[:, :, :HEAD_W].astype(BF16).reshape(nb * HEAD_W, HEAD_W)
            su_scr[d, rows_s, :] = kd_wu[:, :, HEAD_W:].astype(BF16).reshape(nb * HEAD_W, HEAD_W)
            os_scr[d, rows, :] = (q * e_cum - qk_wu[:, :, :HEAD_W]).astype(BF16).reshape(rows_b, HEAD_W)
            ou_scr[d, rows, :] = qk_wu[:, :, HEAD_W:].reshape(rows_b, HEAD_W)
            gl_scr[d, pl.ds(pl.multiple_of(blk * (nb * 8), nb * 8), nb * 8), :] = jnp.broadcast_to(
                jnp.exp(cum_end), (nb, 8, HEAD_W)).reshape(nb * 8, HEAD_W)
        return carry

    lax.fori_loop(0, n_chunks // nb, precompute, 0)

    def scan_step(t, states):
        new_states = []
        for d in range(2):
            n = t if d == 0 else n_chunks - 1 - t
            rows = pl.ds(pl.multiple_of(n * C, C), C)
            rows_s = pl.ds(pl.multiple_of(n * HEAD_W, HEAD_W), HEAD_W)
            st = states[d]
            st16 = st.astype(BF16)
            o_scr[d, rows, :] = ou_scr[d, rows, :] + _dot(os_scr[d, rows, :], st16)
            gl = gl_scr[d, pl.ds(pl.multiple_of(n * 8, 8), 1), :]
            new_states.append(st * gl + su_scr[d, rows_s, :].astype(F32)
                              - _dot(sw_scr[d, rows_s, :], st16))
        return tuple(new_states)

    zero_state = jnp.zeros((HEAD_W, HEAD_W), F32)
    lax.fori_loop(0, n_chunks, scan_step, (zero_state, zero_state))

    for r0 in range(0, seq, conv_rows):
        sl = slice(r0, r0 + conv_rows)
        o = _rms(o_scr[0, sl, :] + o_scr[1, sl, :], nw_ref[...])
        z = z_ref[0, sl, :]
        o_ref[0, sl, :] = (o * (z * jax.nn.sigmoid(z))).astype(o_ref.dtype)


def _gdn(p_main, p_small, conv_w, alf, dtf, alb, dtb, nw):
    b, s, _ = p_main.shape
    smem = pl.BlockSpec(memory_space=pltpu.SMEM)

    def col(blk):
        return pl.BlockSpec((1, s, HEAD_W), lambda i, h: (i, 0, blk + h))

    def cw(blk):
        return pl.BlockSpec((CONV_K, HEAD_W), lambda i, h: (0, blk + h))

    return pl.pallas_call(
        functools.partial(_gdn_kernel, seq=s),
        grid=(b, N_HEADS),
        in_specs=[
            smem, smem, smem, smem,
            col(GQ_BLK), col(GK_BLK), col(GV_BLK), col(Z_BLK),
            pl.BlockSpec((1, s, HEAD_W), lambda i, h: (i, 0, 0)),
            cw(0), cw(N_HEADS), cw(2 * N_HEADS),
            pl.BlockSpec((1, HEAD_W), lambda i, h: (0, 0)),
        ],
        out_specs=pl.BlockSpec((1, s, HEAD_W), lambda i, h: (i, 0, h)),
        out_shape=jax.ShapeDtypeStruct((b, s, N_HEADS * HEAD_W), BF16),
        scratch_shapes=[
            pltpu.VMEM((s + 16, HEAD_W), F32),
            pltpu.VMEM((s, HEAD_W), F32),
            pltpu.VMEM((s, HEAD_W), F32),
            pltpu.VMEM((s, HEAD_W), F32),
            pltpu.VMEM((2, (s // GDN_CHUNK) * HEAD_W, HEAD_W), BF16),
            pltpu.VMEM((2, (s // GDN_CHUNK) * HEAD_W, HEAD_W), BF16),
            pltpu.VMEM((2, s, HEAD_W), BF16),
            pltpu.VMEM((2, s, HEAD_W), F32),
            pltpu.VMEM((2, (s // GDN_CHUNK) * 8, HEAD_W), F32),
            pltpu.VMEM((2, s, HEAD_W), F32),
        ],
        compiler_params=_cparams(("parallel", "arbitrary")),
        name="gdn",
    )(alf, dtf, alb, dtb, p_main, p_main, p_main, p_main, p_small, conv_w, conv_w, conv_w, nw)


def _merge_kernel(x_ref, oa_ref, og_ref, ga_ref, gg_ref, wpa_ref, wpg_ref, wo_ref, n2w_ref,
                  wr_ref, x2_ref, aff_ref):
    ya = _dot(oa_ref[0], wpa_ref[...])
    yg = _dot(og_ref[0], wpg_ref[...])
    merged = jax.nn.sigmoid(ga_ref[0]) * ya + jax.nn.sigmoid(gg_ref[0]) * yg
    x2 = x_ref[0] + _dot(merged.astype(BF16), wo_ref[...])
    tm = x2.shape[0]
    for j in range(LANE_GROUPS):
        x2_ref[0, pl.ds(j, tm, stride=LANE_GROUPS), :] = x2[:, j * HEAD_W:(j + 1) * HEAD_W]
    h2 = _rms(x2, n2w_ref[...])
    logits = _dot(h2, wr_ref[...], HIGHEST)
    lane = lax.broadcasted_iota(jnp.int32, logits.shape, 1)
    valid = lane < N_EXPERTS
    logits = jnp.where(valid, logits, -1e30)
    e = jnp.exp(logits - jnp.max(logits, axis=-1, keepdims=True))
    e = jnp.where(valid, e, 0.0)
    aff_ref[0] = e * (1.0 / jnp.sum(e, axis=-1, keepdims=True))


def _merge(x, oa, og, p_main, wpa, wpg, wo, n2w, wr):
    b, s, d = x.shape
    tm = min(512, s)
    tok = lambda i, m: (i, m, 0)
    const = lambda i, m: (0, 0)
    return pl.pallas_call(
        _merge_kernel,
        grid=(b, s // tm),
        in_specs=[
            pl.BlockSpec((1, tm, d), tok),
            pl.BlockSpec((1, tm, d), tok),
            pl.BlockSpec((1, tm, d), tok),
            pl.BlockSpec((1, tm, d), lambda i, m: (i, m, GATE_A_BLK1024)),
            pl.BlockSpec((1, tm, d), lambda i, m: (i, m, GATE_G_BLK1024)),
            pl.BlockSpec((d, d), const),
            pl.BlockSpec((d, d), const),
            pl.BlockSpec((d, d), const),
            pl.BlockSpec((1, d), const),
            pl.BlockSpec((d, HEAD_W), const),
        ],
        out_specs=[pl.BlockSpec((1, tm * LANE_GROUPS, HEAD_W), tok),
                   pl.BlockSpec((1, tm, HEAD_W), tok)],
        out_shape=[jax.ShapeDtypeStruct((b, s * LANE_GROUPS, HEAD_W), F32),
                   jax.ShapeDtypeStruct((b, s, HEAD_W), F32)],
        compiler_params=_cparams(("parallel", "parallel")),
        name="merge",
    )(x, oa, og, p_main, p_main, wpa, wpg, wo, n2w, wr)


def _route_kernel(aff_ref, idx_ref, val_ref, csm_scr, *, seq, cap):
    blk = 128
    bits = lax.bitcast_convert_type(aff_ref[0], jnp.int32)

    def bisect(i, cur):
        cand = cur | jnp.left_shift(jnp.int32(1), 30 - i)
        cnt = jnp.sum((bits >= cand).astype(F32), axis=0, keepdims=True)
        return jnp.where(cnt >= cap, cand, cur)

    thr = lax.fori_loop(0, 31, bisect, jnp.zeros((1, HEAD_W), jnp.int32))
    n_gt = jnp.sum((bits > thr).astype(F32), axis=0, keepdims=True)
    n_ties = cap - n_gt

    ri = lax.broadcasted_iota(jnp.int32, (blk, blk), 0)
    ci = lax.broadcasted_iota(jnp.int32, (blk, blk), 1)
    ltri = (ci <= ri).astype(BF16)

    def select(j, carry):
        tie_base, sel_base = carry
        rows = pl.ds(pl.multiple_of(j * blk, blk), blk)
        bj = lax.bitcast_convert_type(aff_ref[0, rows, :], jnp.int32)
        eq = jnp.where(bj == thr, 1.0, 0.0)
        tie_rank = _dot(ltri, eq.astype(BF16)) + tie_base - eq
        sel = jnp.where(bj > thr, 1.0, eq * jnp.where(tie_rank < n_ties, 1.0, 0.0))
        cs = _dot(ltri, sel.astype(BF16)) + sel_base
        csm_scr[rows, :] = cs * sel
        return (tie_base + jnp.sum(eq, axis=0, keepdims=True),
                sel_base + jnp.sum(sel, axis=0, keepdims=True))

    zero = jnp.zeros((1, HEAD_W), F32)
    lax.fori_loop(0, seq // blk, select, (zero, zero))

    rb = min(512, seq)
    slot = lax.broadcasted_iota(jnp.int32, (1, cap), 1).astype(F32) + 1.0
    for e in range(N_EXPERTS):
        def gather_slots(j, carry):
            idx_acc, val_acc = carry
            r0 = pl.multiple_of(j * rb, rb)
            rows = pl.ds(r0, rb)
            onehot = csm_scr[rows, e:e + 1] == slot
            tok = (lax.broadcasted_iota(jnp.int32, (rb, 1), 0) + r0).astype(F32)
            idx_acc = idx_acc + jnp.sum(jnp.where(onehot, tok, 0.0), axis=0, keepdims=True)
            val_acc = val_acc + jnp.sum(jnp.where(onehot, aff_ref[0, rows, e:e + 1], 0.0),
                                        axis=0, keepdims=True)
            return idx_acc, val_acc

        zc = jnp.zeros((1, cap), F32)
        idx_e, val_e = lax.fori_loop(0, seq // rb, gather_slots, (zc, zc))
        idx_ref[0, e:e + 1, :] = idx_e.astype(jnp.int32)
        val_ref[0, e:e + 1, :] = val_e


def _route(aff, cap):
    b, s, _ = aff.shape
    return pl.pallas_call(
        functools.partial(_route_kernel, seq=s, cap=cap),
        grid=(b,),
        in_specs=[pl.BlockSpec((1, s, HEAD_W), lambda i: (i, 0, 0))],
        out_specs=[pl.BlockSpec((1, N_EXPERTS, cap), lambda i: (i, 0, 0)),
                   pl.BlockSpec((1, N_EXPERTS, cap), lambda i: (i, 0, 0))],
        out_shape=[jax.ShapeDtypeStruct((b, N_EXPERTS, cap), jnp.int32),
                   jax.ShapeDtypeStruct((b, N_EXPERTS, cap), F32)],
        scratch_shapes=[pltpu.VMEM((s, HEAD_W), F32)],
        compiler_params=_cparams(("parallel",)),
        name="route",
    )(aff)


def _moe_kernel(idx_ref, val_ref, x2_hbm, n2w_ref, wg_ref, wu_ref, wd_ref, out_hbm,
                x2_v, acc_v, xg_v, xs_v, y_v, yt_v, in_sem, acc_sem, out_sem, *, cap, n_ff):
    G = LANE_GROUPS
    b = pl.program_id(0)
    e = pl.program_id(1)
    f = pl.program_id(2)
    slot0 = (b * N_EXPERTS + e) * cap

    def tile_rows(t):
        return pl.ds(pl.multiple_of(t * G, G), G)

    @pl.when((e == 0) & (f == 0))
    def _():
        cp_x = pltpu.make_async_copy(x2_hbm.at[b], x2_v, in_sem)
        cp_a = pltpu.make_async_copy(x2_hbm.at[b], acc_v, acc_sem)
        cp_x.start()
        cp_a.start()
        cp_x.wait()
        cp_a.wait()

    @pl.when(f == 0)
    def _():
        def gather(c, carry):
            xg_v[tile_rows(c), :] = x2_v[tile_rows(idx_ref[slot0 + c]), :]
            return carry

        lax.fori_loop(0, cap, gather, 0, unroll=8)
        groups = [xg_v[pl.ds(j, cap, stride=G), :] for j in range(G)]
        ssq = sum(jnp.sum(g * g, axis=-1, keepdims=True) for g in groups)
        scale = lax.rsqrt(ssq * (1.0 / D_MODEL) + NORM_EPS)
        for j in range(G):
            cols = slice(j * HEAD_W, (j + 1) * HEAD_W)
            xs_v[:, cols] = (groups[j] * scale * n2w_ref[:, cols]).astype(BF16)

    xs = xs_v[...]
    gate = _dot(xs, wg_ref[0])
    up = _dot(xs, wu_ref[0])
    act = (gate * jax.nn.sigmoid(gate) * up).astype(BF16)
    y_part = _dot(act, wd_ref[0])

    @pl.when(f == 0)
    def _():
        y_v[...] = y_part

    @pl.when(f > 0)
    def _():
        y_v[...] += y_part

    @pl.when(f == n_ff - 1)
    def _():
        for j in range(G):
            yt_v[pl.ds(j, cap, stride=G), :] = y_v[:, j * HEAD_W:(j + 1) * HEAD_W]

        def scatter(g, carry):
            c0 = g * SCATTER_BATCH
            dsts = [tile_rows(idx_ref[slot0 + c0 + j]) for j in range(SCATTER_BATCH)]
            new = [acc_v[dsts[j], :] + yt_v[tile_rows(c0 + j), :] * val_ref[slot0 + c0 + j]
                   for j in range(SCATTER_BATCH)]
            for j in range(SCATTER_BATCH):
                acc_v[dsts[j], :] = new[j]
            return carry

        lax.fori_loop(0, cap // SCATTER_BATCH, scatter, 0)

    @pl.when((e == N_EXPERTS - 1) & (f == n_ff - 1))
    def _():
        cp = pltpu.make_async_copy(acc_v, out_hbm.at[b], out_sem)
        cp.start()
        cp.wait()


def _moe(x2t, idx, val, n2w, wg, wu, wd):
    b, rows, _ = x2t.shape
    d = D_MODEL
    cap = idx.shape[-1]
    tf = 1024
    n_ff = EXPERT_FF // tf
    grid_spec = pltpu.PrefetchScalarGridSpec(
        num_scalar_prefetch=2,
        grid=(b, N_EXPERTS, n_ff),
        in_specs=[
            pl.BlockSpec(memory_space=pl.ANY),
            pl.BlockSpec((1, d), lambda i, e, f, *_: (0, 0)),
            pl.BlockSpec((1, d, tf), lambda i, e, f, *_: (e, 0, f)),
            pl.BlockSpec((1, d, tf), lambda i, e, f, *_: (e, 0, f)),
            pl.BlockSpec((1, tf, d), lambda i, e, f, *_: (e, f, 0)),
        ],
        out_specs=pl.BlockSpec(memory_space=pl.ANY),
        scratch_shapes=[
            pltpu.VMEM((rows, HEAD_W), F32),
            pltpu.VMEM((rows, HEAD_W), F32),
            pltpu.VMEM((cap * LANE_GROUPS, HEAD_W), F32),
            pltpu.VMEM((cap, d), BF16),
            pltpu.VMEM((cap, d), F32),
            pltpu.VMEM((cap * LANE_GROUPS, HEAD_W), F32),
            pltpu.SemaphoreType.DMA,
            pltpu.SemaphoreType.DMA,
            pltpu.SemaphoreType.DMA,
        ],
    )
    return pl.pallas_call(
        functools.partial(_moe_kernel, cap=cap, n_ff=n_ff),
        grid_spec=grid_spec,
        out_shape=jax.ShapeDtypeStruct(x2t.shape, F32),
        compiler_params=_cparams(("arbitrary", "arbitrary", "arbitrary"), vmem=60 * 1024 * 1024),
        name="moe",
    )(idx.reshape(-1), val.reshape(-1), x2t, n2w, wg, wu, wd)


def _final_norm_kernel(xt_ref, w_ref, o_ref):
    tm = o_ref.shape[1]
    groups = [xt_ref[0, pl.ds(j, tm, stride=LANE_GROUPS), :] for j in range(LANE_GROUPS)]
    ssq = sum(jnp.sum(g * g, axis=-1, keepdims=True) for g in groups)
    scale = lax.rsqrt(ssq * (1.0 / D_MODEL) + NORM_EPS)
    for j in range(LANE_GROUPS):
        cols = slice(j * HEAD_W, (j + 1) * HEAD_W)
        o_ref[0, :, cols] = groups[j] * scale * w_ref[:, cols]


def _final_norm(xt, w):
    b, rows, _ = xt.shape
    s = rows // LANE_GROUPS
    tm = min(512, s)
    return pl.pallas_call(
        _final_norm_kernel,
        grid=(b, s // tm),
        in_specs=[pl.BlockSpec((1, tm * LANE_GROUPS, HEAD_W), lambda i, m: (i, m, 0)),
                  pl.BlockSpec((1, D_MODEL), lambda i, m: (0, 0))],
        out_specs=pl.BlockSpec((1, tm, D_MODEL), lambda i, m: (i, m, 0)),
        out_shape=jax.ShapeDtypeStruct((b, s, D_MODEL), F32),
        compiler_params=_cparams(("parallel", "parallel")),
        name="final_norm",
    )(xt, w)


def _rope_tables(seq):
    inv = ROPE_THETA ** (-jnp.arange(0, DA_DH, 2, dtype=F32) / DA_DH)
    ang = jnp.arange(seq, dtype=F32)[:, None] * inv[None, :]
    cos, sin = jnp.cos(ang), jnp.sin(ang)
    return (jnp.concatenate([cos, cos, cos, cos], axis=-1),
            jnp.concatenate([-sin, sin, -sin, sin], axis=-1))


def kernel(x, norm1_w, w_in, conv_w, a_log_fwd, dt_bias_fwd, a_log_bwd, dt_bias_bwd, gdn_norm_w,
           lambda_q1, lambda_k1, lambda_q2, lambda_k2, subln_w, w_proj_attn, w_proj_gdn, w_out,
           norm2_w, w_router, w_gate, w_up, w_down, norm_f_w):
    assert w_in.shape[0] == 1, "single-layer block"
    b, s, d = x.shape
    cap = CAP_FACTOR * s // N_EXPERTS
    w = w_in[0]
    small0 = 7 * D_MODEL
    small_w = 4 * N_HEADS
    w_main = jnp.concatenate([w[:, :small0], w[:, small0 + small_w:]], axis=1).astype(BF16)
    w_small = jnp.pad(w[:, small0:small0 + small_w], ((0, 0), (0, HEAD_W - small_w))).astype(BF16)

    p_main, p_small = _in_proj(x, norm1_w, w_main, w_small)

    cos, sin_signed = _rope_tables(s)
    oa = _attention(p_main, cos, sin_signed, lambda_q1, lambda_k1, lambda_q2, lambda_k2, subln_w)
    og = _gdn(p_main, p_small, conv_w[0], a_log_fwd[0], dt_bias_fwd[0], a_log_bwd[0],
              dt_bias_bwd[0], gdn_norm_w)

    w_r = jnp.pad(w_router[0], ((0, 0), (0, HEAD_W - N_EXPERTS)))
    x2, aff = _merge(x, oa, og, p_main, w_proj_attn[0].astype(BF16), w_proj_gdn[0].astype(BF16),
                     w_out[0].astype(BF16), norm2_w, w_r)
    idx, val = _route(aff, cap)
    y = _moe(x2, idx, val, norm2_w, w_gate[0].astype(BF16), w_up[0].astype(BF16),
             w_down[0].astype(BF16))
    return _final_norm(y, norm_f_w.reshape(1, d))
```

```python
import functools
import math

import jax
import jax.numpy as jnp
from jax import lax
from jax.experimental import pallas as pl
from jax.experimental.pallas import tpu as pltpu

F32 = jnp.float32
BF16 = jnp.bfloat16
HIGHEST = lax.Precision.HIGHEST

D_MODEL = 1024
N_HEADS = 8
HEAD_W = 128
LANE_GROUPS = D_MODEL // HEAD_W
DA_DH = 64
ROPE_THETA = 10000.0
CONV_K = 5
GDN_CHUNK = 64
ATTN_KEY_CHUNK = 256
ONES_ROWS = 16
SCATTER_BATCH = 8
N_EXPERTS = 16
EXPERT_FF = 2048
CAP_FACTOR = 2
NORM_EPS = 1e-6
L2_EPS = 1e-6
LAMBDA_INIT = 0.8 - 0.6 * math.exp(-0.3 * 0)

QA_BLK, KA_BLK, VA_BLK = 0, 8, 16
GQ_BLK, GK_BLK, GV_BLK, Z_BLK = 24, 32, 40, 48
GATE_A_BLK1024, GATE_G_BLK1024 = 7, 8
MAIN_W = 9 * D_MODEL

VMEM_LIMIT = 56 * 1024 * 1024


def _cparams(sem, vmem=VMEM_LIMIT):
    return pltpu.CompilerParams(dimension_semantics=sem, vmem_limit_bytes=vmem)


def _dot(a, b, precision=None):
    return jnp.dot(a, b, preferred_element_type=F32, precision=precision)


def _dot_nt(a, b, precision=None):
    return lax.dot_general(a, b, (((1,), (1,)), ((), ())), preferred_element_type=F32,
                           precision=precision)


def _dot_tn(a, b, precision=None):
    return lax.dot_general(a, b, (((0,), (0,)), ((), ())), preferred_element_type=F32,
                           precision=precision)


def _rms(t, w, eps=NORM_EPS):
    return t * lax.rsqrt(jnp.mean(t * t, axis=-1, keepdims=True) + eps) * w


def _softplus(x):
    return jnp.maximum(x, 0.0) + jnp.log1p(jnp.exp(-jnp.abs(x)))


def _inproj_kernel(x_ref, nw_ref, wm_ref, ws_ref, alog_ref, dtb_ref, om_ref, os_ref, h_scr):
    @pl.when(pl.program_id(2) == 0)
    def _():
        h = _rms(x_ref[0], nw_ref[...]).astype(BF16)
        h_scr[...] = h
        logit = _dot(h, ws_ref[...])
        lane = lax.broadcasted_iota(jnp.int32, logit.shape, 1)
        g = -jnp.exp(alog_ref[...]) * _softplus(logit + dtb_ref[...])
        os_ref[0] = jnp.where(lane < 2 * N_HEADS, g, jax.nn.sigmoid(logit))

    om_ref[0] = _dot(h_scr[...], wm_ref[...])


def _in_proj(x, nw, w_main, w_small, a_log_lanes, dt_bias_lanes):
    b, s, d = x.shape
    tm = min(1024, s)
    tn = 1024
    return pl.pallas_call(
        _inproj_kernel,
        grid=(b, s // tm, MAIN_W // tn),
        in_specs=[
            pl.BlockSpec((1, tm, d), lambda i, m, n: (i, m, 0)),
            pl.BlockSpec((1, d), lambda i, m, n: (0, 0)),
            pl.BlockSpec((d, tn), lambda i, m, n: (0, n)),
            pl.BlockSpec((d, HEAD_W), lambda i, m, n: (0, 0)),
            pl.BlockSpec((1, HEAD_W), lambda i, m, n: (0, 0)),
            pl.BlockSpec((1, HEAD_W), lambda i, m, n: (0, 0)),
        ],
        out_specs=[
            pl.BlockSpec((1, tm, tn), lambda i, m, n: (i, m, n)),
            pl.BlockSpec((1, tm, HEAD_W), lambda i, m, n: (i, m, 0)),
        ],
        out_shape=[
            jax.ShapeDtypeStruct((b, s, MAIN_W), F32),
            jax.ShapeDtypeStruct((b, s, HEAD_W), F32),
        ],
        scratch_shapes=[pltpu.VMEM((tm, d), BF16)],
        compiler_params=_cparams(("parallel", "parallel", "arbitrary")),
        name="in_proj",
    )(x, nw, w_main, w_small, a_log_lanes, dt_bias_lanes)


def _attn_kernel(lq1_ref, lk1_ref, lq2_ref, lk2_ref, q_ref, k_ref, v_ref, cos_ref, sin_ref,
                 subw_ref, o_ref, kr_scr, vt_scr, m_scr, *chunk_scr, tq, seq):
    qi = pl.program_id(2)
    n_q = seq // tq
    kc = min(ATTN_KEY_CHUNK, seq)
    n_kc = seq // kc
    slab = 64
    s_scr = (chunk_scr[:n_kc], chunk_scr[n_kc:2 * n_kc])
    p_scr = chunk_scr[2 * n_kc:]
    lane = lax.broadcasted_iota(jnp.int32, (1, HEAD_W), 1)
    first_half = (lane % DA_DH) < (DA_DH // 2)
    lo = lane < DA_DH

    def rope(t, cos, sin_signed):
        rot = jnp.where(first_half, pltpu.roll(t, HEAD_W - DA_DH // 2, 1),
                        pltpu.roll(t, DA_DH // 2, 1))
        return t * cos + rot * sin_signed

    def stacked_queries(tile):
        rows = pl.ds(pl.multiple_of(tile * tq, tq), tq)
        q = rope(q_ref[0, rows, :], cos_ref[rows, :], sin_ref[rows, :]) * (DA_DH ** -0.5)
        return jnp.concatenate([jnp.where(lo, q, 0.0), jnp.where(lo, 0.0, q)], axis=0).astype(BF16)

    def scores_and_max(q2, dst, c, m8):
        dst[...] = _dot_nt(kr_scr[c * kc:(c + 1) * kc, :], q2)
        for r0s in range(0, kc, slab):
            st = dst[r0s:r0s + slab, :]
            m8 = jnp.maximum(m8, jnp.max(st.reshape(slab // 8, 8, 2 * tq), axis=0))
        return m8

    neg = jnp.full((8, 2 * tq), -1e30, F32)

    @pl.when(qi == 0)
    def _():
        kr_scr[...] = rope(k_ref[0], cos_ref[...], sin_ref[...]).astype(BF16)
        for c in range(n_kc):
            vt_scr[c, :HEAD_W, :] = v_ref[0, c * kc:(c + 1) * kc, :].T.astype(BF16)
            vt_scr[c, HEAD_W:, :] = jnp.ones((ONES_ROWS, kc), BF16)
        q2 = stacked_queries(0)
        m8 = neg
        for c in range(n_kc):
            m8 = scores_and_max(q2, s_scr[0][c], c, m8)
        m_scr[...] = m8

    lam = (jnp.exp(jnp.sum(lq1_ref[...] * lk1_ref[...], axis=-1, keepdims=True))
           - jnp.exp(jnp.sum(lq2_ref[...] * lk2_ref[...], axis=-1, keepdims=True))
           + LAMBDA_INIT)

    def tile(parity):
        q2_next = stacked_queries((qi + 1) % n_q)
        m = jnp.max(m_scr[...], axis=0, keepdims=True)
        m8 = neg
        acc = jnp.zeros((HEAD_W + ONES_ROWS, 2 * tq), F32)
        for c in range(n_kc):
            m8 = scores_and_max(q2_next, s_scr[1 - parity][c], c, m8)
            for r0s in range(0, kc, slab):
                p = jnp.exp(s_scr[parity][c][r0s:r0s + slab, :] - m)
                p_scr[c][r0s:r0s + slab, :] = p.astype(BF16)
            acc = acc + _dot(vt_scr[c], p_scr[c][...])
        m_scr[...] = m8
        inv_l = 1.0 / acc[HEAD_W:HEAD_W + 1, :]
        acc = acc[:HEAD_W, :]
        o_t = acc[:, :tq] * inv_l[:, :tq] - lam * (acc[:, tq:] * inv_l[:, tq:])
        o = _rms(o_t.T, subw_ref[...]) * (1.0 - LAMBDA_INIT)
        o_ref[0] = o.astype(o_ref.dtype)

    for parity in range(2):
        pl.when(qi % 2 == parity)(functools.partial(tile, parity))


def _attention(p_main, cos, sin_signed, lq1, lk1, lq2, lk2, subw):
    b, s, _ = p_main.shape
    tq = min(256, s // 2)
    kc = min(ATTN_KEY_CHUNK, s)
    n_kc = s // kc
    vec = pl.BlockSpec((1, DA_DH), lambda i, h, q: (0, 0))

    def col(blk):
        return pl.BlockSpec((1, s, HEAD_W), lambda i, h, q: (i, 0, blk + h))

    return pl.pallas_call(
        functools.partial(_attn_kernel, tq=tq, seq=s),
        grid=(b, N_HEADS, s // tq),
        in_specs=[
            vec, vec, vec, vec,
            col(QA_BLK), col(KA_BLK), col(VA_BLK),
            pl.BlockSpec((s, HEAD_W), lambda i, h, q: (0, 0)),
            pl.BlockSpec((s, HEAD_W), lambda i, h, q: (0, 0)),
            pl.BlockSpec((1, HEAD_W), lambda i, h, q: (0, 0)),
        ],
        out_specs=pl.BlockSpec((1, tq, HEAD_W), lambda i, h, q: (i, q, h)),
        out_shape=jax.ShapeDtypeStruct((b, s, N_HEADS * HEAD_W), BF16),
        scratch_shapes=([pltpu.VMEM((s, HEAD_W), BF16),
                         pltpu.VMEM((n_kc, HEAD_W + ONES_ROWS, kc), BF16),
                         pltpu.VMEM((8, 2 * tq), F32)]
                        + [pltpu.VMEM((kc, 2 * tq), F32)] * (2 * n_kc)
                        + [pltpu.VMEM((kc, 2 * tq), BF16)] * n_kc),
        compiler_params=_cparams(("parallel", "parallel", "arbitrary")),
        name="diff_attn",
    )(lq1, lk1, lq2, lk2, p_main, p_main, p_main, cos, sin_signed, subw)


def _gdn_kernel(q_ref, k_ref, v_ref, z_ref, sm_ref, cwq_ref, cwk_ref, cwv_ref, nw_ref,
                o_ref,
                pad_scr, q_scr, k_scr, v_scr, sw_scr, su_scr, os_scr, ou_scr, gl_scr,
                o_scr, *, seq):
    C = GDN_CHUNK
    n_chunks = seq // C
    h = pl.program_id(1)
    halo = 8
    conv_rows = min(512, seq)

    zeros_halo = jnp.zeros((halo, HEAD_W), F32)
    pad_scr[0:halo, :] = zeros_halo
    pad_scr[halo + seq:2 * halo + seq, :] = zeros_halo

    def conv_silu(src_ref, cw_ref, dst_scr, normalize):
        pad_scr[halo:halo + seq, :] = src_ref[0]
        cw = cw_ref[...]
        for r0 in range(0, seq, conv_rows):
            acc = None
            for t in range(CONV_K):
                start = halo + r0 + t - CONV_K // 2
                term = cw[t:t + 1, :] * pad_scr[start:start + conv_rows, :]
                acc = term if acc is None else acc + term
            y = acc * jax.nn.sigmoid(acc)
            if normalize:
                y = y * lax.rsqrt(jnp.sum(y * y, axis=-1, keepdims=True) + L2_EPS)
            dst_scr[r0:r0 + conv_rows, :] = y

    conv_silu(q_ref, cwq_ref, q_scr, True)
    conv_silu(k_ref, cwk_ref, k_scr, True)
    conv_silu(v_ref, cwv_ref, v_scr, False)

    ri = lax.broadcasted_iota(jnp.int32, (C, C), 0)
    ci = lax.broadcasted_iota(jnp.int32, (C, C), 1)
    incl = (ci <= ri, ci >= ri)
    strict = (ci < ri, ci > ri)
    lane = lax.broadcasted_iota(jnp.int32, (1, HEAD_W), 1)
    dk_scale = HEAD_W ** -0.5

    def lane_pick(t, idx):
        return jnp.sum(jnp.where(lane == idx, t, 0.0), axis=-1, keepdims=True)

    def split_bf16(t, terms):
        parts = []
        for _ in range(terms - 1):
            p = t.astype(BF16)
            parts.append(p)
            t = t - p.astype(F32)
        parts.append(t.astype(BF16))
        return parts

    def bdot(a, b):
        return lax.dot_general(a, b, (((2,), (1,)), ((0,), (0,))), preferred_element_type=F32)

    def bdot_nt(a, b):
        return lax.dot_general(a, b, (((2,), (2,)), ((0,), (0,))), preferred_element_type=F32)

    def bdot_tn(a, b):
        return lax.dot_general(a, b, (((1,), (1,)), ((0,), (0,))), preferred_element_type=F32)

    def unit_tri_inverse_minus_eye(a):
        a16 = a.astype(BF16)
        pw = bdot(a16, a16)
        r = -a
        for step in range(int(math.log2(C)) - 1):
            if step > 0:
                pw16 = pw.astype(BF16)
                pw = bdot(pw16, pw16)
            r = r + pw + bdot(r.astype(BF16), pw.astype(BF16))
        return r

    nb = min(16, n_chunks)
    rows_b = nb * C
    incl16 = tuple(jnp.broadcast_to(m.astype(BF16), (nb, C, C)) for m in incl)
    LANE_GROUP_W = HEAD_W // 4
    lane_c = lax.broadcasted_iota(jnp.int32, (C, HEAD_W), 1)
    group_first_lane16 = jnp.broadcast_to(
        ((lane_c % LANE_GROUP_W == 0) & (lane_c < 3 * LANE_GROUP_W)).astype(BF16), (nb, C, HEAD_W))

    def by_lane_group(parts):
        p0, p1, p2 = (p.astype(F32) for p in parts)
        return jnp.where(lane_c < LANE_GROUP_W, p0,
                         jnp.where(lane_c < 2 * LANE_GROUP_W, p1,
                                   jnp.where(lane_c < 3 * LANE_GROUP_W, p2, 0.0))).astype(BF16)

    def precompute(blk, carry):
        rows = pl.ds(pl.multiple_of(blk * rows_b, rows_b), rows_b)
        q = (q_scr[rows, :] * dk_scale).reshape(nb, C, HEAD_W)
        k = k_scr[rows, :].reshape(nb, C, HEAD_W)
        v = v_scr[rows, :].reshape(nb, C, HEAD_W)
        sm = sm_ref[0, rows, :].reshape(nb, C, HEAD_W)
        k16 = k.astype(BF16)
        qk_raw = bdot_nt(q.astype(BF16), k16)
        for d in range(2):
            g = lane_pick(sm, d * N_HEADS + h)
            beta = lane_pick(sm, (2 + d) * N_HEADS + h)
            cum_terms = bdot(incl16[d], by_lane_group(split_bf16(jnp.broadcast_to(g, (nb, C, HEAD_W)), 3)))
            cum = cum_terms
            for shift in (LANE_GROUP_W, 2 * LANE_GROUP_W, 3 * LANE_GROUP_W):
                cum = cum + pltpu.roll(cum_terms, shift, 2)
            cum_i = cum[:, :, :C]
            cum_j = bdot_nt(group_first_lane16, by_lane_group(split_bf16(cum, 3)))
            decay = jnp.where(incl[d], jnp.exp(jnp.where(incl[d], cum_i - cum_j, 0.0)), 0.0)
            k_beta = k * beta
            a_mat = jnp.where(strict[d], bdot_nt(k_beta.astype(BF16), k16) * decay, 0.0)
            r16 = unit_tri_inverse_minus_eye(a_mat).astype(BF16)
            e_cum = jnp.exp(cum)
            v_beta = v * beta
            kb_dec = k_beta * e_cum
            r_rhs = bdot(r16, jnp.concatenate([v_beta.astype(BF16), kb_dec.astype(BF16)], axis=-1))
            u = v_beta + r_rhs[:, :, :HEAD_W]
            w = kb_dec + r_rhs[:, :, HEAD_W:]
            cum_end = cum[:, C - 1:C, :] if d == 0 else cum[:, 0:1, :]
            wu16 = jnp.concatenate([w.astype(BF16), u.astype(BF16)], axis=-1)
            kd16 = (k * jnp.exp(cum_end - cum)).astype(BF16)
            qk16 = jnp.where(incl[d], qk_raw * decay, 0.0).astype(BF16)
            kd_wu = bdot_tn(kd16, wu16)
            qk_wu = bdot(qk16, wu16)
            rows_s = pl.ds(pl.multiple_of(blk * (nb * HEAD_W), nb * HEAD_W), nb * HEAD_W)
            sw_scr[d, rows_s, :] = kd_wu[:, :, :HEAD_W].astype(BF16).reshape(nb * HEAD_W, HEAD_W)
            su_scr[d, rows_s, :] = kd_wu[:, :, HEAD_W:].astype(BF16).reshape(nb * HEAD_W, HEAD_W)
            os_scr[d, rows, :] = (q * e_cum - qk_wu[:, :, :HEAD_W]).astype(BF16).reshape(rows_b, HEAD_W)
            ou_scr[d, rows, :] = qk_wu[:, :, HEAD_W:].reshape(rows_b, HEAD_W)
            gl_scr[d, pl.ds(pl.multiple_of(blk * (nb * 8), nb * 8), nb * 8), :] = jnp.broadcast_to(
                jnp.exp(cum_end), (nb, 8, HEAD_W)).reshape(nb * 8, HEAD_W)
        return carry

    lax.fori_loop(0, n_chunks // nb, precompute, 0)

    def scan_step(t, states):
        new_states = []
        for d in range(2):
            n = t if d == 0 else n_chunks - 1 - t
            rows = pl.ds(pl.multiple_of(n * C, C), C)
            rows_s = pl.ds(pl.multiple_of(n * HEAD_W, HEAD_W), HEAD_W)
            st = states[d]
            st16 = st.astype(BF16)
            o_scr[d, rows, :] = ou_scr[d, rows, :] + _dot(os_scr[d, rows, :], st16)
            gl = gl_scr[d, pl.ds(pl.multiple_of(n * 8, 8), 1), :]
            new_states.append(st * gl + su_scr[d, rows_s, :].astype(F32)
                              - _dot(sw_scr[d, rows_s, :], st16))
        return tuple(new_states)

    zero_state = jnp.zeros((HEAD_W, HEAD_W), F32)
    lax.fori_loop(0, n_chunks, scan_step, (zero_state, zero_state))

    for r0 in range(0, seq, conv_rows):
        sl = slice(r0, r0 + conv_rows)
        o = _rms(o_scr[0, sl, :] + o_scr[1, sl, :], nw_ref[...])
        z = z_ref[0, sl, :]
        o_ref[0, sl, :] = (o * (z * jax.nn.sigmoid(z))).astype(o_ref.dtype)


def _gdn(p_main, gates, conv_w, nw):
    b, s, _ = p_main.shape

    def col(blk):
        return pl.BlockSpec((1, s, HEAD_W), lambda i, h: (i, 0, blk + h))

    def cw(blk):
        return pl.BlockSpec((CONV_K, HEAD_W), lambda i, h: (0, blk + h))

    return pl.pallas_call(
        functools.partial(_gdn_kernel, seq=s),
        grid=(b, N_HEADS),
        in_specs=[
            col(GQ_BLK), col(GK_BLK), col(GV_BLK), col(Z_BLK),
            pl.BlockSpec((1, s, HEAD_W), lambda i, h: (i, 0, 0)),
            cw(0), cw(N_HEADS), cw(2 * N_HEADS),
            pl.BlockSpec((1, HEAD_W), lambda i, h: (0, 0)),
        ],
        out_specs=pl.BlockSpec((1, s, HEAD_W), lambda i, h: (i, 0, h)),
        out_shape=jax.ShapeDtypeStruct((b, s, N_HEADS * HEAD_W), BF16),
        scratch_shapes=[
            pltpu.VMEM((s + 16, HEAD_W), F32),
            pltpu.VMEM((s, HEAD_W), F32),
            pltpu.VMEM((s, HEAD_W), F32),
            pltpu.VMEM((s, HEAD_W), F32),
            pltpu.VMEM((2, (s // GDN_CHUNK) * HEAD_W, HEAD_W), BF16),
            pltpu.VMEM((2, (s // GDN_CHUNK) * HEAD_W, HEAD_W), BF16),
            pltpu.VMEM((2, s, HEAD_W), BF16),
            pltpu.VMEM((2, s, HEAD_W), F32),
            pltpu.VMEM((2, (s // GDN_CHUNK) * 8, HEAD_W), F32),
            pltpu.VMEM((2, s, HEAD_W), F32),
        ],
        compiler_params=_cparams(("parallel", "arbitrary")),
        name="gdn",
    )(p_main, p_main, p_main, p_main, gates, conv_w, conv_w, conv_w, nw)


def _merge_kernel(x_ref, oa_ref, og_ref, ga_ref, gg_ref, wpa_ref, wpg_ref, wo_ref, n2w_ref,
                  wr_ref, x2_ref, aff_ref):
    ya = _dot(oa_ref[0], wpa_ref[...])
    yg = _dot(og_ref[0], wpg_ref[...])
    merged = jax.nn.sigmoid(ga_ref[0]) * ya + jax.nn.sigmoid(gg_ref[0]) * yg
    x2 = x_ref[0] + _dot(merged.astype(BF16), wo_ref[...])
    tm = x2.shape[0]
    for j in range(LANE_GROUPS):
        x2_ref[0, pl.ds(j, tm, stride=LANE_GROUPS), :] = x2[:, j * HEAD_W:(j + 1) * HEAD_W]
    h2 = _rms(x2, n2w_ref[...])
    logits = _dot(h2, wr_ref[...], HIGHEST)
    lane = lax.broadcasted_iota(jnp.int32, logits.shape, 1)
    valid = lane < N_EXPERTS
    logits = jnp.where(valid, logits, -1e30)
    e = jnp.exp(logits - jnp.max(logits, axis=-1, keepdims=True))
    e = jnp.where(valid, e, 0.0)
    aff_ref[0] = e * (1.0 / jnp.sum(e, axis=-1, keepdims=True))


def _merge(x, oa, og, p_main, wpa, wpg, wo, n2w, wr):
    b, s, d = x.shape
    tm = min(512, s)
    tok = lambda i, m: (i, m, 0)
    const = lambda i, m: (0, 0)
    return pl.pallas_call(
        _merge_kernel,
        grid=(b, s // tm),
        in_specs=[
            pl.BlockSpec((1, tm, d), tok),
            pl.BlockSpec((1, tm, d), tok),
            pl.BlockSpec((1, tm, d), tok),
            pl.BlockSpec((1, tm, d), lambda i, m: (i, m, GATE_A_BLK1024)),
            pl.BlockSpec((1, tm, d), lambda i, m: (i, m, GATE_G_BLK1024)),
            pl.BlockSpec((d, d), const),
            pl.BlockSpec((d, d), const),
            pl.BlockSpec((d, d), const),
            pl.BlockSpec((1, d), const),
            pl.BlockSpec((d, HEAD_W), const),
        ],
        out_specs=[pl.BlockSpec((1, tm * LANE_GROUPS, HEAD_W), tok),
                   pl.BlockSpec((1, tm, HEAD_W), tok)],
        out_shape=[jax.ShapeDtypeStruct((b, s * LANE_GROUPS, HEAD_W), F32),
                   jax.ShapeDtypeStruct((b, s, HEAD_W), F32)],
        compiler_params=_cparams(("parallel", "parallel")),
        name="merge",
    )(x, oa, og, p_main, p_main, wpa, wpg, wo, n2w, wr)


def _route_kernel(aff_ref, idx_ref, val_ref, csm_scr, *, seq, cap):
    blk = 128
    bits = lax.bitcast_convert_type(aff_ref[0], jnp.int32)

    def bisect(i, cur):
        cand = cur | jnp.left_shift(jnp.int32(1), 30 - i)
        cnt = jnp.sum((bits >= cand).astype(F32), axis=0, keepdims=True)
        return jnp.where(cnt >= cap, cand, cur)

    thr = lax.fori_loop(0, 31, bisect, jnp.zeros((1, HEAD_W), jnp.int32))
    n_gt = jnp.sum((bits > thr).astype(F32), axis=0, keepdims=True)
    n_ties = cap - n_gt

    ri = lax.broadcasted_iota(jnp.int32, (blk, blk), 0)
    ci = lax.broadcasted_iota(jnp.int32, (blk, blk), 1)
    ltri = (ci <= ri).astype(BF16)

    def select(j, carry):
        tie_base, sel_base = carry
        rows = pl.ds(pl.multiple_of(j * blk, blk), blk)
        bj = lax.bitcast_convert_type(aff_ref[0, rows, :], jnp.int32)
        eq = jnp.where(bj == thr, 1.0, 0.0)
        tie_rank = _dot(ltri, eq.astype(BF16)) + tie_base - eq
        sel = jnp.where(bj > thr, 1.0, eq * jnp.where(tie_rank < n_ties, 1.0, 0.0))
        cs = _dot(ltri, sel.astype(BF16)) + sel_base
        csm_scr[rows, :] = cs * sel
        return (tie_base + jnp.sum(eq, axis=0, keepdims=True),
                sel_base + jnp.sum(sel, axis=0, keepdims=True))

    zero = jnp.zeros((1, HEAD_W), F32)
    lax.fori_loop(0, seq // blk, select, (zero, zero))

    rb = min(512, seq)
    slot = lax.broadcasted_iota(jnp.int32, (1, cap), 1).astype(F32) + 1.0
    for e in range(N_EXPERTS):
        def gather_slots(j, carry):
            idx_acc, val_acc = carry
            r0 = pl.multiple_of(j * rb, rb)
            rows = pl.ds(r0, rb)
            onehot = csm_scr[rows, e:e + 1] == slot
            tok = (lax.broadcasted_iota(jnp.int32, (rb, 1), 0) + r0).astype(F32)
            idx_acc = idx_acc + jnp.sum(jnp.where(onehot, tok, 0.0), axis=0, keepdims=True)
            val_acc = val_acc + jnp.sum(jnp.where(onehot, aff_ref[0, rows, e:e + 1], 0.0),
                                        axis=0, keepdims=True)
            return idx_acc, val_acc

        zc = jnp.zeros((1, cap), F32)
        idx_e, val_e = lax.fori_loop(0, seq // rb, gather_slots, (zc, zc))
        idx_ref[0, e:e + 1, :] = idx_e.astype(jnp.int32)
        val_ref[0, e:e + 1, :] = val_e


def _route(aff, cap):
    b, s, _ = aff.shape
    return pl.pallas_call(
        functools.partial(_route_kernel, seq=s, cap=cap),
        grid=(b,),
        in_specs=[pl.BlockSpec((1, s, HEAD_W), lambda i: (i, 0, 0))],
        out_specs=[pl.BlockSpec((1, N_EXPERTS, cap), lambda i: (i, 0, 0)),
                   pl.BlockSpec((1, N_EXPERTS, cap), lambda i: (i, 0, 0))],
        out_shape=[jax.ShapeDtypeStruct((b, N_EXPERTS, cap), jnp.int32),
                   jax.ShapeDtypeStruct((b, N_EXPERTS, cap), F32)],
        scratch_shapes=[pltpu.VMEM((s, HEAD_W), F32)],
        compiler_params=_cparams(("parallel",)),
        name="route",
    )(aff)


def _moe_kernel(idx_ref, val_ref, x2_hbm, n2w_ref, wg_ref, wu_ref, wd_ref, out_hbm,
                x2_v, acc_v, xg_v, xs_v, y_v, yt_v, in_sem, acc_sem, out_sem, *, cap, n_ff):
    G = LANE_GROUPS
    b = pl.program_id(0)
    e = pl.program_id(1)
    f = pl.program_id(2)
    slot0 = (b * N_EXPERTS + e) * cap

    def tile_rows(t):
        return pl.ds(pl.multiple_of(t * G, G), G)

    @pl.when((e == 0) & (f == 0))
    def _():
        cp_x = pltpu.make_async_copy(x2_hbm.at[b], x2_v, in_sem)
        cp_a = pltpu.make_async_copy(x2_hbm.at[b], acc_v, acc_sem)
        cp_x.start()
        cp_a.start()
        cp_x.wait()
        cp_a.wait()

    @pl.when(f == 0)
    def _():
        def gather(c, carry):
            xg_v[tile_rows(c), :] = x2_v[tile_rows(idx_ref[slot0 + c]), :]
            return carry

        lax.fori_loop(0, cap, gather, 0, unroll=8)
        groups = [xg_v[pl.ds(j, cap, stride=G), :] for j in range(G)]
        ssq = sum(jnp.sum(g * g, axis=-1, keepdims=True) for g in groups)
        scale = lax.rsqrt(ssq * (1.0 / D_MODEL) + NORM_EPS)
        for j in range(G):
            cols = slice(j * HEAD_W, (j + 1) * HEAD_W)
            xs_v[:, cols] = (groups[j] * scale * n2w_ref[:, cols]).astype(BF16)

    xs = xs_v[...]
    gate = _dot(xs, wg_ref[0])
    up = _dot(xs, wu_ref[0])
    act = (gate * jax.nn.sigmoid(gate) * up).astype(BF16)
    y_part = _dot(act, wd_ref[0])

    @pl.when(f == 0)
    def _():
        y_v[...] = y_part

    @pl.when(f > 0)
    def _():
        y_v[...] += y_part

    @pl.when(f == n_ff - 1)
    def _():
        for j in range(G):
            yt_v[pl.ds(j, cap, stride=G), :] = y_v[:, j * HEAD_W:(j + 1) * HEAD_W]

        def scatter(g, carry):
            c0 = g * SCATTER_BATCH
            dsts = [tile_rows(idx_ref[slot0 + c0 + j]) for j in range(SCATTER_BATCH)]
            new = [acc_v[dsts[j], :] + yt_v[tile_rows(c0 + j), :] * val_ref[slot0 + c0 + j]
                   for j in range(SCATTER_BATCH)]
            for j in range(SCATTER_BATCH):
                acc_v[dsts[j], :] = new[j]
            return carry

        lax.fori_loop(0, cap // SCATTER_BATCH, scatter, 0)

    @pl.when((e == N_EXPERTS - 1) & (f == n_ff - 1))
    def _():
        cp = pltpu.make_async_copy(acc_v, out_hbm.at[b], out_sem)
        cp.start()
        cp.wait()


def _moe(x2t, idx, val, n2w, wg, wu, wd):
    b, rows, _ = x2t.shape
    d = D_MODEL
    cap = idx.shape[-1]
    tf = 1024
    n_ff = EXPERT_FF // tf
    grid_spec = pltpu.PrefetchScalarGridSpec(
        num_scalar_prefetch=2,
        grid=(b, N_EXPERTS, n_ff),
        in_specs=[
            pl.BlockSpec(memory_space=pl.ANY),
            pl.BlockSpec((1, d), lambda i, e, f, *_: (0, 0)),
            pl.BlockSpec((1, d, tf), lambda i, e, f, *_: (e, 0, f)),
            pl.BlockSpec((1, d, tf), lambda i, e, f, *_: (e, 0, f)),
            pl.BlockSpec((1, tf, d), lambda i, e, f, *_: (e, f, 0)),
        ],
        out_specs=pl.BlockSpec(memory_space=pl.ANY),
        scratch_shapes=[
            pltpu.VMEM((rows, HEAD_W), F32),
            pltpu.VMEM((rows, HEAD_W), F32),
            pltpu.VMEM((cap * LANE_GROUPS, HEAD_W), F32),
            pltpu.VMEM((cap, d), BF16),
            pltpu.VMEM((cap, d), F32),
            pltpu.VMEM((cap * LANE_GROUPS, HEAD_W), F32),
            pltpu.SemaphoreType.DMA,
            pltpu.SemaphoreType.DMA,
            pltpu.SemaphoreType.DMA,
        ],
    )
    return pl.pallas_call(
        functools.partial(_moe_kernel, cap=cap, n_ff=n_ff),
        grid_spec=grid_spec,
        out_shape=jax.ShapeDtypeStruct(x2t.shape, F32),
        compiler_params=_cparams(("arbitrary", "arbitrary", "arbitrary"), vmem=60 * 1024 * 1024),
        name="moe",
    )(idx.reshape(-1), val.reshape(-1), x2t, n2w, wg, wu, wd)


def _final_norm_kernel(xt_ref, w_ref, o_ref):
    tm = o_ref.shape[1]
    groups = [xt_ref[0, pl.ds(j, tm, stride=LANE_GROUPS), :] for j in range(LANE_GROUPS)]
    ssq = sum(jnp.sum(g * g, axis=-1, keepdims=True) for g in groups)
    scale = lax.rsqrt(ssq * (1.0 / D_MODEL) + NORM_EPS)
    for j in range(LANE_GROUPS):
        cols = slice(j * HEAD_W, (j + 1) * HEAD_W)
        o_ref[0, :, cols] = groups[j] * scale * w_ref[:, cols]


def _final_norm(xt, w):
    b, rows, _ = xt.shape
    s = rows // LANE_GROUPS
    tm = min(512, s)
    return pl.pallas_call(
        _final_norm_kernel,
        grid=(b, s // tm),
        in_specs=[pl.BlockSpec((1, tm * LANE_GROUPS, HEAD_W), lambda i, m: (i, m, 0)),
                  pl.BlockSpec((1, D_MODEL), lambda i, m: (0, 0))],
        out_specs=pl.BlockSpec((1, tm, D_MODEL), lambda i, m: (i, m, 0)),
        out_shape=jax.ShapeDtypeStruct((b, s, D_MODEL), F32),
        compiler_params=_cparams(("parallel", "parallel")),
        name="final_norm",
    )(xt, w)


def _rope_tables(seq):
    inv = ROPE_THETA ** (-jnp.arange(0, DA_DH, 2, dtype=F32) / DA_DH)
    ang = jnp.arange(seq, dtype=F32)[:, None] * inv[None, :]
    cos, sin = jnp.cos(ang), jnp.sin(ang)
    return (jnp.concatenate([cos, cos, cos, cos], axis=-1),
            jnp.concatenate([-sin, sin, -sin, sin], axis=-1))


def kernel(x, norm1_w, w_in, conv_w, a_log_fwd, dt_bias_fwd, a_log_bwd, dt_bias_bwd, gdn_norm_w,
           lambda_q1, lambda_k1, lambda_q2, lambda_k2, subln_w, w_proj_attn, w_proj_gdn, w_out,
           norm2_w, w_router, w_gate, w_up, w_down, norm_f_w):
    assert w_in.shape[0] == 1, "single-layer block"
    b, s, d = x.shape
    cap = CAP_FACTOR * s // N_EXPERTS
    w = w_in[0]
    small0 = 7 * D_MODEL
    small_w = 4 * N_HEADS
    w_main = jnp.concatenate([w[:, :small0], w[:, small0 + small_w:]], axis=1).astype(BF16)
    w_small = jnp.pad(w[:, small0:small0 + small_w], ((0, 0), (0, HEAD_W - small_w))).astype(BF16)

    pad_lanes = jnp.zeros((HEAD_W - 2 * N_HEADS,), F32)
    a_log_lanes = jnp.concatenate([a_log_fwd[0], a_log_bwd[0], pad_lanes]).reshape(1, HEAD_W)
    dt_bias_lanes = jnp.concatenate([dt_bias_fwd[0], dt_bias_bwd[0], pad_lanes]).reshape(1, HEAD_W)
    p_main, gates = _in_proj(x, norm1_w, w_main, w_small, a_log_lanes, dt_bias_lanes)

    cos, sin_signed = _rope_tables(s)
    oa = _attention(p_main, cos, sin_signed, lambda_q1, lambda_k1, lambda_q2, lambda_k2, subln_w)
    og = _gdn(p_main, gates, conv_w[0], gdn_norm_w)

    w_r = jnp.pad(w_router[0], ((0, 0), (0, HEAD_W - N_EXPERTS)))
    x2, aff = _merge(x, oa, og, p_main, w_proj_attn[0].astype(BF16), w_proj_gdn[0].astype(BF16),
                     w_out[0].astype(BF16), norm2_w, w_r)
    idx, val = _route(aff, cap)
    y = _moe(x2, idx, val, norm2_w, w_gate[0].astype(BF16), w_up[0].astype(BF16),
             w_down[0].astype(BF16))
    return _final_norm(y, norm_f_w.reshape(1, d))
```

```python
import functools
import math

import jax
import jax.numpy as jnp
from jax import lax
from jax.experimental import pallas as pl
from jax.experimental.pallas import tpu as pltpu

F32 = jnp.float32
BF16 = jnp.bfloat16
HIGHEST = lax.Precision.HIGHEST

D_MODEL = 1024
N_HEADS = 8
HEAD_W = 128
LANE_GROUPS = D_MODEL // HEAD_W
DA_DH = 64
ROPE_THETA = 10000.0
CONV_K = 5
GDN_CHUNK = 64
ATTN_KEY_CHUNK = 256
ONES_ROWS = 16
SCATTER_BATCH = 8
N_EXPERTS = 16
EXPERT_FF = 2048
CAP_FACTOR = 2
NORM_EPS = 1e-6
L2_EPS = 1e-6
LAMBDA_INIT = 0.8 - 0.6 * math.exp(-0.3 * 0)

QA_BLK, KA_BLK, VA_BLK = 0, 8, 16
GQ_BLK, GK_BLK, GV_BLK, Z_BLK = 24, 32, 40, 48
GATE_A_BLK1024, GATE_G_BLK1024 = 7, 8
MAIN_W = 9 * D_MODEL

VMEM_LIMIT = 56 * 1024 * 1024
VMEM_LIMIT_LARGE = 60 * 1024 * 1024


def _cparams(sem, vmem=VMEM_LIMIT):
    return pltpu.CompilerParams(dimension_semantics=sem, vmem_limit_bytes=vmem)


def _dot(a, b, precision=None):
    return jnp.dot(a, b, preferred_element_type=F32, precision=precision)


def _dot_nt(a, b, precision=None):
    return lax.dot_general(a, b, (((1,), (1,)), ((), ())), preferred_element_type=F32,
                           precision=precision)


def _dot_tn(a, b, precision=None):
    return lax.dot_general(a, b, (((0,), (0,)), ((), ())), preferred_element_type=F32,
                           precision=precision)


def _rms(t, w, eps=NORM_EPS):
    return t * lax.rsqrt(jnp.mean(t * t, axis=-1, keepdims=True) + eps) * w


def _softplus(x):
    return jnp.maximum(x, 0.0) + jnp.log1p(jnp.exp(-jnp.abs(x)))


def _inproj_kernel(x_ref, nw_ref, wm_ref, ws_ref, alog_ref, dtb_ref, om_ref, os_ref, h_scr):
    @pl.when(pl.program_id(2) == 0)
    def _():
        h = _rms(x_ref[0], nw_ref[...]).astype(BF16)
        h_scr[...] = h
        logit = _dot(h, ws_ref[...])
        lane = lax.broadcasted_iota(jnp.int32, logit.shape, 1)
        g = -jnp.exp(alog_ref[...]) * _softplus(logit + dtb_ref[...])
        os_ref[0] = jnp.where(lane < 2 * N_HEADS, g, jax.nn.sigmoid(logit))

    om_ref[0] = _dot(h_scr[...], wm_ref[...])


def _in_proj(x, nw, w_main, w_small, a_log_lanes, dt_bias_lanes):
    b, s, d = x.shape
    tm = min(1024, s)
    tn = 1024
    return pl.pallas_call(
        _inproj_kernel,
        grid=(b, s // tm, MAIN_W // tn),
        in_specs=[
            pl.BlockSpec((1, tm, d), lambda i, m, n: (i, m, 0)),
            pl.BlockSpec((1, d), lambda i, m, n: (0, 0)),
            pl.BlockSpec((d, tn), lambda i, m, n: (0, n)),
            pl.BlockSpec((d, HEAD_W), lambda i, m, n: (0, 0)),
            pl.BlockSpec((1, HEAD_W), lambda i, m, n: (0, 0)),
            pl.BlockSpec((1, HEAD_W), lambda i, m, n: (0, 0)),
        ],
        out_specs=[
            pl.BlockSpec((1, tm, tn), lambda i, m, n: (i, m, n)),
            pl.BlockSpec((1, tm, HEAD_W), lambda i, m, n: (i, m, 0)),
        ],
        out_shape=[
            jax.ShapeDtypeStruct((b, s, MAIN_W), F32),
            jax.ShapeDtypeStruct((b, s, HEAD_W), F32),
        ],
        scratch_shapes=[pltpu.VMEM((tm, d), BF16)],
        compiler_params=_cparams(("parallel", "parallel", "arbitrary")),
        name="in_proj",
    )(x, nw, w_main, w_small, a_log_lanes, dt_bias_lanes)


def _attn_kernel(lq1_ref, lk1_ref, lq2_ref, lk2_ref, q_ref, k_ref, v_ref, cos_ref, sin_ref,
                 subw_ref, o_ref, kr_scr, vt_scr, q2_scr, m_scr, acc_scr, *chunk_scr, tq, seq):
    qi = pl.program_id(2)
    n_q = seq // tq
    kc = min(ATTN_KEY_CHUNK, seq)
    n_kc = seq // kc
    slab = 64
    s_scr = (chunk_scr[:n_kc], chunk_scr[n_kc:2 * n_kc])
    p_scr = chunk_scr[2 * n_kc:]
    lane = lax.broadcasted_iota(jnp.int32, (1, HEAD_W), 1)
    first_half = (lane % DA_DH) < (DA_DH // 2)
    lo = lane < DA_DH

    def rope(t, cos, sin_signed):
        rot = jnp.where(first_half, pltpu.roll(t, HEAD_W - DA_DH // 2, 1),
                        pltpu.roll(t, DA_DH // 2, 1))
        return t * cos + rot * sin_signed

    def stacked_queries(tile):
        rows = slice(tile * tq, (tile + 1) * tq)
        q = rope(q_ref[0, rows, :], cos_ref[rows, :], sin_ref[rows, :]) * (DA_DH ** -0.5)
        return jnp.concatenate([jnp.where(lo, q, 0.0), jnp.where(lo, 0.0, q)], axis=0).astype(BF16)

    def scores_and_max(q2, dst, c, m8):
        dst[...] = _dot_nt(kr_scr[c * kc:(c + 1) * kc, :], q2)
        for r0s in range(0, kc, slab):
            st = dst[r0s:r0s + slab, :]
            m8 = jnp.maximum(m8, jnp.max(st.reshape(slab // 8, 8, 2 * tq), axis=0))
        return m8

    lam = (jnp.exp(jnp.sum(lq1_ref[...] * lk1_ref[...], axis=-1, keepdims=True))
           - jnp.exp(jnp.sum(lq2_ref[...] * lk2_ref[...], axis=-1, keepdims=True))
           + LAMBDA_INIT)

    def finish(tile):
        inv_l = 1.0 / acc_scr[HEAD_W:HEAD_W + 1, :]
        o_t = (acc_scr[:HEAD_W, :tq] * inv_l[:, :tq]
               - lam * (acc_scr[:HEAD_W, tq:] * inv_l[:, tq:]))
        o = _rms(o_t.T, subw_ref[...]) * (1.0 - LAMBDA_INIT)
        o_ref[0, pl.ds(pl.multiple_of(tile * tq, tq), tq), :] = o.astype(o_ref.dtype)

    neg = jnp.full((8, 2 * tq), -1e30, F32)

    @pl.when(qi == 0)
    def _():
        kr_scr[...] = rope(k_ref[0], cos_ref[...], sin_ref[...]).astype(BF16)
        for c in range(n_kc):
            vt_scr[c, :HEAD_W, :] = v_ref[0, c * kc:(c + 1) * kc, :].T.astype(BF16)
            vt_scr[c, HEAD_W:, :] = jnp.ones((ONES_ROWS, kc), BF16)
        q2 = stacked_queries(0)
        q2_scr[0] = q2
        m8 = neg
        per_chunk = -(-(n_q - 1) // n_kc)
        for c in range(n_kc):
            m8 = scores_and_max(q2, s_scr[0][c], c, m8)
            for t in range(1 + c * per_chunk, min(n_q, 1 + (c + 1) * per_chunk)):
                q2_scr[t] = stacked_queries(t)
        m_scr[...] = m8
        acc_scr[...] = jnp.ones(acc_scr.shape, F32)

    def tile(parity):
        q2_next = q2_scr[(qi + 1) % n_q]
        m = jnp.max(m_scr[...], axis=0, keepdims=True)
        m8 = neg
        acc = jnp.zeros((HEAD_W + ONES_ROWS, 2 * tq), F32)
        for c in range(n_kc):
            m8 = scores_and_max(q2_next, s_scr[1 - parity][c], c, m8)
            for r0s in range(0, kc, slab):
                p = jnp.exp(s_scr[parity][c][r0s:r0s + slab, :] - m)
                p_scr[c][r0s:r0s + slab, :] = p.astype(BF16)
            acc = acc + _dot(vt_scr[c], p_scr[c][...])
            if c == 0:
                finish((qi + n_q - 1) % n_q)
        m_scr[...] = m8
        acc_scr[...] = acc

    for parity in range(2):
        pl.when(qi % 2 == parity)(functools.partial(tile, parity))

    @pl.when(qi == n_q - 1)
    def _():
        finish(qi)


def _attention(p_main, cos, sin_signed, lq1, lk1, lq2, lk2, subw):
    b, s, _ = p_main.shape
    tq = min(256, s // 2)
    kc = min(ATTN_KEY_CHUNK, s)
    n_kc = s // kc
    vec = pl.BlockSpec((1, DA_DH), lambda i, h, q: (0, 0))

    def col(blk):
        return pl.BlockSpec((1, s, HEAD_W), lambda i, h, q: (i, 0, blk + h))

    return pl.pallas_call(
        functools.partial(_attn_kernel, tq=tq, seq=s),
        grid=(b, N_HEADS, s // tq),
        in_specs=[
            vec, vec, vec, vec,
            col(QA_BLK), col(KA_BLK), col(VA_BLK),
            pl.BlockSpec((s, HEAD_W), lambda i, h, q: (0, 0)),
            pl.BlockSpec((s, HEAD_W), lambda i, h, q: (0, 0)),
            pl.BlockSpec((1, HEAD_W), lambda i, h, q: (0, 0)),
        ],
        out_specs=pl.BlockSpec((1, s, HEAD_W), lambda i, h, q: (i, 0, h)),
        out_shape=jax.ShapeDtypeStruct((b, s, N_HEADS * HEAD_W), BF16),
        scratch_shapes=([pltpu.VMEM((s, HEAD_W), BF16),
                         pltpu.VMEM((n_kc, HEAD_W + ONES_ROWS, kc), BF16),
                         pltpu.VMEM((s // tq, 2 * tq, HEAD_W), BF16),
                         pltpu.VMEM((8, 2 * tq), F32),
                         pltpu.VMEM((HEAD_W + ONES_ROWS, 2 * tq), F32)]
                        + [pltpu.VMEM((kc, 2 * tq), F32)] * (2 * n_kc)
                        + [pltpu.VMEM((kc, 2 * tq), BF16)] * n_kc),
        compiler_params=_cparams(("parallel", "parallel", "arbitrary")),
        name="diff_attn",
    )(lq1, lk1, lq2, lk2, p_main, p_main, p_main, cos, sin_signed, subw)


def _gdn_kernel(q_ref, k_ref, v_ref, z_ref, sm_ref, cwq_ref, cwk_ref, cwv_ref, nw_ref,
                o_ref,
                pad_scr, q_scr, k_scr, v_scr, sw_scr, su_scr, os_scr, ou_scr, gl_scr,
                o_scr, *, seq):
    C = GDN_CHUNK
    n_chunks = seq // C
    h = pl.program_id(1)
    halo = 8
    conv_rows = min(512, seq)

    zeros_halo = jnp.zeros((halo, HEAD_W), F32)
    pad_scr[0:halo, :] = zeros_halo
    pad_scr[halo + seq:2 * halo + seq, :] = zeros_halo

    def conv_silu(src_ref, cw_ref, dst_scr, normalize):
        pad_scr[halo:halo + seq, :] = src_ref[0]
        cw = cw_ref[...]
        for r0 in range(0, seq, conv_rows):
            acc = None
            for t in range(CONV_K):
                start = halo + r0 + t - CONV_K // 2
                term = cw[t:t + 1, :] * pad_scr[start:start + conv_rows, :]
                acc = term if acc is None else acc + term
            y = acc * jax.nn.sigmoid(acc)
            if normalize:
                y = y * lax.rsqrt(jnp.sum(y * y, axis=-1, keepdims=True) + L2_EPS)
            dst_scr[r0:r0 + conv_rows, :] = y

    conv_silu(q_ref, cwq_ref, q_scr, True)
    conv_silu(k_ref, cwk_ref, k_scr, True)
    conv_silu(v_ref, cwv_ref, v_scr, False)

    ri = lax.broadcasted_iota(jnp.int32, (C, C), 0)
    ci = lax.broadcasted_iota(jnp.int32, (C, C), 1)
    incl = (ci <= ri, ci >= ri)
    strict = (ci < ri, ci > ri)
    lane = lax.broadcasted_iota(jnp.int32, (1, HEAD_W), 1)
    dk_scale = HEAD_W ** -0.5

    def lane_pick(t, idx):
        return jnp.sum(jnp.where(lane == idx, t, 0.0), axis=-1, keepdims=True)

    def split_bf16(t, terms):
        parts = []
        for _ in range(terms - 1):
            p = t.astype(BF16)
            parts.append(p)
            t = t - p.astype(F32)
        parts.append(t.astype(BF16))
        return parts

    def bdot(a, b):
        return lax.dot_general(a, b, (((2,), (1,)), ((0,), (0,))), preferred_element_type=F32)

    def bdot_nt(a, b):
        return lax.dot_general(a, b, (((2,), (2,)), ((0,), (0,))), preferred_element_type=F32)

    def bdot_tn(a, b):
        return lax.dot_general(a, b, (((1,), (1,)), ((0,), (0,))), preferred_element_type=F32)

    nb = min(16, n_chunks)
    rows_b = nb * C
    incl16 = tuple(jnp.broadcast_to(m.astype(BF16), (nb, C, C)) for m in incl)
    LANE_GROUP_W = HEAD_W // 4
    lane_c = lax.broadcasted_iota(jnp.int32, (C, HEAD_W), 1)
    group_first_lane16 = jnp.broadcast_to(
        ((lane_c % LANE_GROUP_W == 0) & (lane_c < 3 * LANE_GROUP_W)).astype(BF16), (nb, C, HEAD_W))

    def by_lane_group(parts):
        p0, p1, p2 = (p.astype(F32) for p in parts)
        return jnp.where(lane_c < LANE_GROUP_W, p0,
                         jnp.where(lane_c < 2 * LANE_GROUP_W, p1,
                                   jnp.where(lane_c < 3 * LANE_GROUP_W, p2, 0.0))).astype(BF16)

    def precompute(blk, carry):
        rows = pl.ds(pl.multiple_of(blk * rows_b, rows_b), rows_b)
        q = (q_scr[rows, :] * dk_scale).reshape(nb, C, HEAD_W)
        k = k_scr[rows, :].reshape(nb, C, HEAD_W)
        v = v_scr[rows, :].reshape(nb, C, HEAD_W)
        sm = sm_ref[0, rows, :].reshape(nb, C, HEAD_W)
        k16 = k.astype(BF16)
        qk_raw = bdot_nt(q.astype(BF16), k16)
        dirs = range(2)
        beta, cum, decay, k_beta, a_mat = [], [], [], [], []
        for d in dirs:
            g = lane_pick(sm, d * N_HEADS + h)
            beta.append(lane_pick(sm, (2 + d) * N_HEADS + h))
            cum_terms = bdot(incl16[d], by_lane_group(split_bf16(jnp.broadcast_to(g, (nb, C, HEAD_W)), 3)))
            cum_d = cum_terms
            for shift in (LANE_GROUP_W, 2 * LANE_GROUP_W, 3 * LANE_GROUP_W):
                cum_d = cum_d + pltpu.roll(cum_terms, shift, 2)
            cum.append(cum_d)
            cum_i = cum_d[:, :, :C]
            cum_j = bdot_nt(group_first_lane16, by_lane_group(split_bf16(cum_d, 3)))
            decay.append(jnp.where(incl[d], jnp.exp(jnp.where(incl[d], cum_i - cum_j, 0.0)), 0.0))
            k_beta.append(k * beta[d])
            a_mat.append(jnp.where(strict[d], bdot_nt(k_beta[d].astype(BF16), k16) * decay[d], 0.0))

        a16 = [a_mat[d].astype(BF16) for d in dirs]
        pw = [bdot(a16[d], a16[d]) for d in dirs]
        r = [-a_mat[d] for d in dirs]
        for step in range(int(math.log2(C)) - 1):
            if step > 0:
                pw16 = [pw[d].astype(BF16) for d in dirs]
                pw = [bdot(pw16[d], pw16[d]) for d in dirs]
            r = [r[d] + pw[d] + bdot(r[d].astype(BF16), pw[d].astype(BF16)) for d in dirs]

        for d in dirs:
            r16 = r[d].astype(BF16)
            e_cum = jnp.exp(cum[d])
            v_beta = v * beta[d]
            kb_dec = k_beta[d] * e_cum
            r_rhs = bdot(r16, jnp.concatenate([v_beta.astype(BF16), kb_dec.astype(BF16)], axis=-1))
            u = v_beta + r_rhs[:, :, :HEAD_W]
            w = kb_dec + r_rhs[:, :, HEAD_W:]
            cum_end = cum[d][:, C - 1:C, :] if d == 0 else cum[d][:, 0:1, :]
            wu16 = jnp.concatenate([w.astype(BF16), u.astype(BF16)], axis=-1)
            kd16 = (k * jnp.exp(cum_end - cum[d])).astype(BF16)
            qk16 = jnp.where(incl[d], qk_raw * decay[d], 0.0).astype(BF16)
            kd_wu = bdot_tn(kd16, wu16)
            qk_wu = bdot(qk16, wu16)
            rows_s = pl.ds(pl.multiple_of(blk * (nb * HEAD_W), nb * HEAD_W), nb * HEAD_W)
            sw_scr[d, rows_s, :] = kd_wu[:, :, :HEAD_W].astype(BF16).reshape(nb * HEAD_W, HEAD_W)
            su_scr[d, rows_s, :] = kd_wu[:, :, HEAD_W:].astype(BF16).reshape(nb * HEAD_W, HEAD_W)
            os_scr[d, rows, :] = (q * e_cum - qk_wu[:, :, :HEAD_W]).astype(BF16).reshape(rows_b, HEAD_W)
            ou_scr[d, rows, :] = qk_wu[:, :, HEAD_W:].reshape(rows_b, HEAD_W)
            gl_scr[d, pl.ds(pl.multiple_of(blk * (nb * 8), nb * 8), nb * 8), :] = jnp.broadcast_to(
                jnp.exp(cum_end), (nb, 8, HEAD_W)).reshape(nb * 8, HEAD_W)
        return carry

    lax.fori_loop(0, n_chunks // nb, precompute, 0)

    def scan_step(t, states):
        new_states = []
        for d in range(2):
            n = t if d == 0 else n_chunks - 1 - t
            rows = pl.ds(pl.multiple_of(n * C, C), C)
            rows_s = pl.ds(pl.multiple_of(n * HEAD_W, HEAD_W), HEAD_W)
            st = states[d]
            st16 = st.astype(BF16)
            o_scr[d, rows, :] = ou_scr[d, rows, :] + _dot(os_scr[d, rows, :], st16)
            gl = gl_scr[d, pl.ds(pl.multiple_of(n * 8, 8), 1), :]
            new_states.append(st * gl + su_scr[d, rows_s, :].astype(F32)
                              - _dot(sw_scr[d, rows_s, :], st16))
        return tuple(new_states)

    zero_state = jnp.zeros((HEAD_W, HEAD_W), F32)
    lax.fori_loop(0, n_chunks, scan_step, (zero_state, zero_state))

    for r0 in range(0, seq, conv_rows):
        sl = slice(r0, r0 + conv_rows)
        o = _rms(o_scr[0, sl, :] + o_scr[1, sl, :], nw_ref[...])
        z = z_ref[0, sl, :]
        o_ref[0, sl, :] = (o * (z * jax.nn.sigmoid(z))).astype(o_ref.dtype)


def _gdn(p_main, gates, conv_w, nw):
    b, s, _ = p_main.shape

    def col(blk):
        return pl.BlockSpec((1, s, HEAD_W), lambda i, h: (i, 0, blk + h))

    def cw(blk):
        return pl.BlockSpec((CONV_K, HEAD_W), lambda i, h: (0, blk + h))

    return pl.pallas_call(
        functools.partial(_gdn_kernel, seq=s),
        grid=(b, N_HEADS),
        in_specs=[
            col(GQ_BLK), col(GK_BLK), col(GV_BLK), col(Z_BLK),
            pl.BlockSpec((1, s, HEAD_W), lambda i, h: (i, 0, 0)),
            cw(0), cw(N_HEADS), cw(2 * N_HEADS),
            pl.BlockSpec((1, HEAD_W), lambda i, h: (0, 0)),
        ],
        out_specs=pl.BlockSpec((1, s, HEAD_W), lambda i, h: (i, 0, h)),
        out_shape=jax.ShapeDtypeStruct((b, s, N_HEADS * HEAD_W), BF16),
        scratch_shapes=[
            pltpu.VMEM((s + 16, HEAD_W), F32),
            pltpu.VMEM((s, HEAD_W), F32),
            pltpu.VMEM((s, HEAD_W), F32),
            pltpu.VMEM((s, HEAD_W), F32),
            pltpu.VMEM((2, (s // GDN_CHUNK) * HEAD_W, HEAD_W), BF16),
            pltpu.VMEM((2, (s // GDN_CHUNK) * HEAD_W, HEAD_W), BF16),
            pltpu.VMEM((2, s, HEAD_W), BF16),
            pltpu.VMEM((2, s, HEAD_W), F32),
            pltpu.VMEM((2, (s // GDN_CHUNK) * 8, HEAD_W), F32),
            pltpu.VMEM((2, s, HEAD_W), F32),
        ],
        compiler_params=_cparams(("parallel", "arbitrary"), vmem=VMEM_LIMIT_LARGE),
        name="gdn",
    )(p_main, p_main, p_main, p_main, gates, conv_w, conv_w, conv_w, nw)


def _merge_kernel(x_ref, oa_ref, og_ref, ga_ref, gg_ref, wpa_ref, wpg_ref, wo_ref, n2w_ref,
                  wr_ref, x2_ref, aff_ref):
    ya = _dot(oa_ref[0], wpa_ref[...])
    yg = _dot(og_ref[0], wpg_ref[...])
    merged = jax.nn.sigmoid(ga_ref[0]) * ya + jax.nn.sigmoid(gg_ref[0]) * yg
    x2 = x_ref[0] + _dot(merged.astype(BF16), wo_ref[...])
    tm = x2.shape[0]
    for j in range(LANE_GROUPS):
        x2_ref[0, pl.ds(j, tm, stride=LANE_GROUPS), :] = x2[:, j * HEAD_W:(j + 1) * HEAD_W]
    h2 = _rms(x2, n2w_ref[...])
    logits = _dot(h2, wr_ref[...], HIGHEST)
    lane = lax.broadcasted_iota(jnp.int32, logits.shape, 1)
    valid = lane < N_EXPERTS
    logits = jnp.where(valid, logits, -1e30)
    e = jnp.exp(logits - jnp.max(logits, axis=-1, keepdims=True))
    e = jnp.where(valid, e, 0.0)
    aff_ref[0] = e * (1.0 / jnp.sum(e, axis=-1, keepdims=True))


def _merge(x, oa, og, p_main, wpa, wpg, wo, n2w, wr):
    b, s, d = x.shape
    tm = min(512, s)
    tok = lambda i, m: (i, m, 0)
    const = lambda i, m: (0, 0)
    return pl.pallas_call(
        _merge_kernel,
        grid=(b, s // tm),
        in_specs=[
            pl.BlockSpec((1, tm, d), tok),
            pl.BlockSpec((1, tm, d), tok),
            pl.BlockSpec((1, tm, d), tok),
            pl.BlockSpec((1, tm, d), lambda i, m: (i, m, GATE_A_BLK1024)),
            pl.BlockSpec((1, tm, d), lambda i, m: (i, m, GATE_G_BLK1024)),
            pl.BlockSpec((d, d), const),
            pl.BlockSpec((d, d), const),
            pl.BlockSpec((d, d), const),
            pl.BlockSpec((1, d), const),
            pl.BlockSpec((d, HEAD_W), const),
        ],
        out_specs=[pl.BlockSpec((1, tm * LANE_GROUPS, HEAD_W), tok),
                   pl.BlockSpec((1, tm, HEAD_W), tok)],
        out_shape=[jax.ShapeDtypeStruct((b, s * LANE_GROUPS, HEAD_W), F32),
                   jax.ShapeDtypeStruct((b, s, HEAD_W), F32)],
        compiler_params=_cparams(("parallel", "parallel")),
        name="merge",
    )(x, oa, og, p_main, p_main, wpa, wpg, wo, n2w, wr)


def _route_kernel(aff_ref, idx_ref, val_ref, csm_scr, *, seq, cap):
    blk = 128
    bits = lax.bitcast_convert_type(aff_ref[0], jnp.int32)

    groups = HEAD_W // N_EXPERTS
    packed = aff_ref[0, pl.ds(0, seq // groups, stride=groups), :]
    for g in range(1, groups):
        packed = packed + pltpu.roll(aff_ref[0, pl.ds(g, seq // groups, stride=groups), :],
                                     g * N_EXPERTS, 1)
    bits_packed = lax.bitcast_convert_type(packed, jnp.int32)

    def over_groups(cnt):
        shift = N_EXPERTS
        while shift < HEAD_W:
            cnt = cnt + pltpu.roll(cnt, shift, 1)
            shift *= 2
        return cnt

    def bisect(i, cur):
        cand = cur | jnp.left_shift(jnp.int32(1), 30 - i)
        cnt = over_groups(jnp.sum((bits_packed >= cand).astype(F32), axis=0, keepdims=True))
        return jnp.where(cnt >= cap, cand, cur)

    thr = lax.fori_loop(0, 31, bisect, jnp.zeros((1, HEAD_W), jnp.int32))
    n_gt = jnp.sum((bits > thr).astype(F32), axis=0, keepdims=True)
    n_ties = cap - n_gt

    ri = lax.broadcasted_iota(jnp.int32, (blk, blk), 0)
    ci = lax.broadcasted_iota(jnp.int32, (blk, blk), 1)
    ltri = (ci <= ri).astype(BF16)

    def select(j, carry):
        tie_base, sel_base = carry
        rows = pl.ds(pl.multiple_of(j * blk, blk), blk)
        bj = lax.bitcast_convert_type(aff_ref[0, rows, :], jnp.int32)
        eq = jnp.where(bj == thr, 1.0, 0.0)
        tie_rank = _dot(ltri, eq.astype(BF16)) + tie_base - eq
        sel = jnp.where(bj > thr, 1.0, eq * jnp.where(tie_rank < n_ties, 1.0, 0.0))
        cs = _dot(ltri, sel.astype(BF16)) + sel_base
        csm_scr[rows, :] = cs * sel
        return (tie_base + jnp.sum(eq, axis=0, keepdims=True),
                sel_base + jnp.sum(sel, axis=0, keepdims=True))

    zero = jnp.zeros((1, HEAD_W), F32)
    lax.fori_loop(0, seq // blk, select, (zero, zero))

    rb = min(512, seq)
    slot = lax.broadcasted_iota(jnp.int32, (1, cap), 1).astype(F32) + 1.0
    for e in range(N_EXPERTS):
        def gather_slots(j, carry):
            idx_acc, val_acc = carry
            r0 = pl.multiple_of(j * rb, rb)
            rows = pl.ds(r0, rb)
            onehot = csm_scr[rows, e:e + 1] == slot
            tok = (lax.broadcasted_iota(jnp.int32, (rb, 1), 0) + r0).astype(F32)
            idx_acc = idx_acc + jnp.sum(jnp.where(onehot, tok, 0.0), axis=0, keepdims=True)
            val_acc = val_acc + jnp.sum(jnp.where(onehot, aff_ref[0, rows, e:e + 1], 0.0),
                                        axis=0, keepdims=True)
            return idx_acc, val_acc

        zc = jnp.zeros((1, cap), F32)
        idx_e, val_e = lax.fori_loop(0, seq // rb, gather_slots, (zc, zc))
        idx_ref[0, e:e + 1, :] = idx_e.astype(jnp.int32)
        val_ref[0, e:e + 1, :] = val_e


def _route(aff, cap):
    b, s, _ = aff.shape
    return pl.pallas_call(
        functools.partial(_route_kernel, seq=s, cap=cap),
        grid=(b,),
        in_specs=[pl.BlockSpec((1, s, HEAD_W), lambda i: (i, 0, 0))],
        out_specs=[pl.BlockSpec((1, N_EXPERTS, cap), lambda i: (i, 0, 0)),
                   pl.BlockSpec((1, N_EXPERTS, cap), lambda i: (i, 0, 0))],
        out_shape=[jax.ShapeDtypeStruct((b, N_EXPERTS, cap), jnp.int32),
                   jax.ShapeDtypeStruct((b, N_EXPERTS, cap), F32)],
        scratch_shapes=[pltpu.VMEM((s, HEAD_W), F32)],
        compiler_params=_cparams(("parallel",)),
        name="route",
    )(aff)


def _moe_kernel(idx_ref, val_ref, x2_hbm, n2w_ref, wg_ref, wu_ref, wd_ref, out_hbm,
                x2_v, acc_v, xg_v, xs_v, y_v, yt_v, in_sem, acc_sem, out_sem, *, cap, n_ff):
    G = LANE_GROUPS
    b = pl.program_id(0)
    e = pl.program_id(1)
    f = pl.program_id(2)
    slot0 = (b * N_EXPERTS + e) * cap

    def tile_rows(t):
        return pl.ds(pl.multiple_of(t * G, G), G)

    @pl.when((e == 0) & (f == 0))
    def _():
        cp_x = pltpu.make_async_copy(x2_hbm.at[b], x2_v, in_sem)
        cp_a = pltpu.make_async_copy(x2_hbm.at[b], acc_v, acc_sem)
        cp_x.start()
        cp_a.start()
        cp_x.wait()
        cp_a.wait()

    @pl.when(f == 0)
    def _():
        def gather(c, carry):
            xg_v[tile_rows(c), :] = x2_v[tile_rows(idx_ref[slot0 + c]), :]
            return carry

        lax.fori_loop(0, cap, gather, 0, unroll=8)
        groups = [xg_v[pl.ds(j, cap, stride=G), :] for j in range(G)]
        ssq = sum(jnp.sum(g * g, axis=-1, keepdims=True) for g in groups)
        scale = lax.rsqrt(ssq * (1.0 / D_MODEL) + NORM_EPS)
        for j in range(G):
            cols = slice(j * HEAD_W, (j + 1) * HEAD_W)
            xs_v[:, cols] = (groups[j] * scale * n2w_ref[:, cols]).astype(BF16)

    xs = xs_v[...]
    gate = _dot(xs, wg_ref[0])
    up = _dot(xs, wu_ref[0])
    act = (gate * jax.nn.sigmoid(gate) * up).astype(BF16)
    y_part = _dot(act, wd_ref[0])

    @pl.when(f == 0)
    def _():
        y_v[...] = y_part

    @pl.when(f > 0)
    def _():
        y_v[...] += y_part

    @pl.when(f == n_ff - 1)
    def _():
        for j in range(G):
            yt_v[pl.ds(j, cap, stride=G), :] = y_v[:, j * HEAD_W:(j + 1) * HEAD_W]

        def scatter(g, carry):
            c0 = g * SCATTER_BATCH
            dsts = [tile_rows(idx_ref[slot0 + c0 + j]) for j in range(SCATTER_BATCH)]
            new = [acc_v[dsts[j], :] + yt_v[tile_rows(c0 + j), :] * val_ref[slot0 + c0 + j]
                   for j in range(SCATTER_BATCH)]
            for j in range(SCATTER_BATCH):
                acc_v[dsts[j], :] = new[j]
            return carry

        lax.fori_loop(0, cap // SCATTER_BATCH, scatter, 0)

    @pl.when((e == N_EXPERTS - 1) & (f == n_ff - 1))
    def _():
        cp = pltpu.make_async_copy(acc_v, out_hbm.at[b], out_sem)
        cp.start()
        cp.wait()


def _moe(x2t, idx, val, n2w, wg, wu, wd):
    b, rows, _ = x2t.shape
    d = D_MODEL
    cap = idx.shape[-1]
    tf = 1024
    n_ff = EXPERT_FF // tf
    grid_spec = pltpu.PrefetchScalarGridSpec(
        num_scalar_prefetch=2,
        grid=(b, N_EXPERTS, n_ff),
        in_specs=[
            pl.BlockSpec(memory_space=pl.ANY),
            pl.BlockSpec((1, d), lambda i, e, f, *_: (0, 0)),
            pl.BlockSpec((1, d, tf), lambda i, e, f, *_: (e, 0, f)),
            pl.BlockSpec((1, d, tf), lambda i, e, f, *_: (e, 0, f)),
            pl.BlockSpec((1, tf, d), lambda i, e, f, *_: (e, f, 0)),
        ],
        out_specs=pl.BlockSpec(memory_space=pl.ANY),
        scratch_shapes=[
            pltpu.VMEM((rows, HEAD_W), F32),
            pltpu.VMEM((rows, HEAD_W), F32),
            pltpu.VMEM((cap * LANE_GROUPS, HEAD_W), F32),
            pltpu.VMEM((cap, d), BF16),
            pltpu.VMEM((cap, d), F32),
            pltpu.VMEM((cap * LANE_GROUPS, HEAD_W), F32),
            pltpu.SemaphoreType.DMA,
            pltpu.SemaphoreType.DMA,
            pltpu.SemaphoreType.DMA,
        ],
    )
    return pl.pallas_call(
        functools.partial(_moe_kernel, cap=cap, n_ff=n_ff),
        grid_spec=grid_spec,
        out_shape=jax.ShapeDtypeStruct(x2t.shape, F32),
        compiler_params=_cparams(("arbitrary", "arbitrary", "arbitrary"), vmem=VMEM_LIMIT_LARGE),
        name="moe",
    )(idx.reshape(-1), val.reshape(-1), x2t, n2w, wg, wu, wd)


def _final_norm_kernel(xt_ref, w_ref, o_ref):
    tm = o_ref.shape[1]
    groups = [xt_ref[0, pl.ds(j, tm, stride=LANE_GROUPS), :] for j in range(LANE_GROUPS)]
    ssq = sum(jnp.sum(g * g, axis=-1, keepdims=True) for g in groups)
    scale = lax.rsqrt(ssq * (1.0 / D_MODEL) + NORM_EPS)
    for j in range(LANE_GROUPS):
        cols = slice(j * HEAD_W, (j + 1) * HEAD_W)
        o_ref[0, :, cols] = groups[j] * scale * w_ref[:, cols]


def _final_norm(xt, w):
    b, rows, _ = xt.shape
    s = rows // LANE_GROUPS
    tm = min(512, s)
    return pl.pallas_call(
        _final_norm_kernel,
        grid=(b, s // tm),
        in_specs=[pl.BlockSpec((1, tm * LANE_GROUPS, HEAD_W), lambda i, m: (i, m, 0)),
                  pl.BlockSpec((1, D_MODEL), lambda i, m: (0, 0))],
        out_specs=pl.BlockSpec((1, tm, D_MODEL), lambda i, m: (i, m, 0)),
        out_shape=jax.ShapeDtypeStruct((b, s, D_MODEL), F32),
        compiler_params=_cparams(("parallel", "parallel")),
        name="final_norm",
    )(xt, w)


def _rope_tables(seq):
    inv = ROPE_THETA ** (-jnp.arange(0, DA_DH, 2, dtype=F32) / DA_DH)
    ang = jnp.arange(seq, dtype=F32)[:, None] * inv[None, :]
    cos, sin = jnp.cos(ang), jnp.sin(ang)
    return (jnp.concatenate([cos, cos, cos, cos], axis=-1),
            jnp.concatenate([-sin, sin, -sin, sin], axis=-1))


def kernel(x, norm1_w, w_in, conv_w, a_log_fwd, dt_bias_fwd, a_log_bwd, dt_bias_bwd, gdn_norm_w,
           lambda_q1, lambda_k1, lambda_q2, lambda_k2, subln_w, w_proj_attn, w_proj_gdn, w_out,
           norm2_w, w_router, w_gate, w_up, w_down, norm_f_w):
    assert w_in.shape[0] == 1, "single-layer block"
    b, s, d = x.shape
    cap = CAP_FACTOR * s // N_EXPERTS
    w = w_in[0]
    small0 = 7 * D_MODEL
    small_w = 4 * N_HEADS
    w_main = jnp.concatenate([w[:, :small0], w[:, small0 + small_w:]], axis=1).astype(BF16)
    w_small = jnp.pad(w[:, small0:small0 + small_w], ((0, 0), (0, HEAD_W - small_w))).astype(BF16)

    pad_lanes = jnp.zeros((HEAD_W - 2 * N_HEADS,), F32)
    a_log_lanes = jnp.concatenate([a_log_fwd[0], a_log_bwd[0], pad_lanes]).reshape(1, HEAD_W)
    dt_bias_lanes = jnp.concatenate([dt_bias_fwd[0], dt_bias_bwd[0], pad_lanes]).reshape(1, HEAD_W)
    p_main, gates = _in_proj(x, norm1_w, w_main, w_small, a_log_lanes, dt_bias_lanes)

    cos, sin_signed = _rope_tables(s)
    oa = _attention(p_main, cos, sin_signed, lambda_q1, lambda_k1, lambda_q2, lambda_k2, subln_w)
    og = _gdn(p_main, gates, conv_w[0], gdn_norm_w)

    w_r = jnp.pad(w_router[0], ((0, 0), (0, HEAD_W - N_EXPERTS)))
    x2, aff = _merge(x, oa, og, p_main, w_proj_attn[0].astype(BF16), w_proj_gdn[0].astype(BF16),
                     w_out[0].astype(BF16), norm2_w, w_r)
    idx, val = _route(aff, cap)
    y = _moe(x2, idx, val, norm2_w, w_gate[0].astype(BF16), w_up[0].astype(BF16),
             w_down[0].astype(BF16))
    return _final_norm(y, norm_f_w.reshape(1, d))
```

```python
import functools
import math

import jax
import jax.numpy as jnp
from jax import lax
from jax.experimental import pallas as pl
from jax.experimental.pallas import tpu as pltpu

F32 = jnp.float32
BF16 = jnp.bfloat16

D_MODEL = 1024
N_HEADS = 8
HEAD_W = 128
LANE_GROUPS = D_MODEL // HEAD_W
DA_DH = 64
ROPE_THETA = 10000.0
CONV_K = 5
GDN_CHUNK = 64
ATTN_KEY_CHUNK = 256
ONES_ROWS = 16
SCATTER_BATCH = 8
N_EXPERTS = 16
EXPERT_FF = 2048
CAP_FACTOR = 2
NORM_EPS = 1e-6
L2_EPS = 1e-6
LAMBDA_INIT = 0.8 - 0.6 * math.exp(-0.3 * 0)

QA_BLK, KA_BLK, VA_BLK = 0, 8, 16
GQ_BLK, GK_BLK, GV_BLK, Z_BLK = 24, 32, 40, 48
GATE_A_BLK1024, GATE_G_BLK1024 = 7, 8
MAIN_W = 9 * D_MODEL

VMEM_LIMIT = 56 * 1024 * 1024
VMEM_LIMIT_LARGE = 60 * 1024 * 1024


def _cparams(sem, vmem=VMEM_LIMIT):
    return pltpu.CompilerParams(dimension_semantics=sem, vmem_limit_bytes=vmem)


def _dot(a, b, precision=None):
    return jnp.dot(a, b, preferred_element_type=F32, precision=precision)


def _dot_nt(a, b, precision=None):
    return lax.dot_general(a, b, (((1,), (1,)), ((), ())), preferred_element_type=F32,
                           precision=precision)


def _dot_tn(a, b, precision=None):
    return lax.dot_general(a, b, (((0,), (0,)), ((), ())), preferred_element_type=F32,
                           precision=precision)


def _rms(t, w, eps=NORM_EPS):
    return t * lax.rsqrt(jnp.mean(t * t, axis=-1, keepdims=True) + eps) * w


def _softplus(x):
    return jnp.maximum(x, 0.0) + jnp.log1p(jnp.exp(-jnp.abs(x)))


def _inproj_kernel(x_ref, nw_ref, wm_ref, ws_ref, alog_ref, dtb_ref, om_ref, os_ref, h_scr):
    @pl.when(pl.program_id(2) == 0)
    def _():
        h = _rms(x_ref[0], nw_ref[...]).astype(BF16)
        h_scr[...] = h
        logit = _dot(h, ws_ref[...])
        lane = lax.broadcasted_iota(jnp.int32, logit.shape, 1)
        g = -jnp.exp(alog_ref[...]) * _softplus(logit + dtb_ref[...])
        os_ref[0] = jnp.where(lane < 2 * N_HEADS, g, jax.nn.sigmoid(logit))

    om_ref[0] = _dot(h_scr[...], wm_ref[...])


def _in_proj(x, nw, w_main, w_small, a_log_lanes, dt_bias_lanes):
    b, s, d = x.shape
    tm = min(1024, s)
    tn = 1024
    return pl.pallas_call(
        _inproj_kernel,
        grid=(b, s // tm, MAIN_W // tn),
        in_specs=[
            pl.BlockSpec((1, tm, d), lambda i, m, n: (i, m, 0)),
            pl.BlockSpec((1, d), lambda i, m, n: (0, 0)),
            pl.BlockSpec((d, tn), lambda i, m, n: (0, n)),
            pl.BlockSpec((d, HEAD_W), lambda i, m, n: (0, 0)),
            pl.BlockSpec((1, HEAD_W), lambda i, m, n: (0, 0)),
            pl.BlockSpec((1, HEAD_W), lambda i, m, n: (0, 0)),
        ],
        out_specs=[
            pl.BlockSpec((1, tm, tn), lambda i, m, n: (i, m, n)),
            pl.BlockSpec((1, tm, HEAD_W), lambda i, m, n: (i, m, 0)),
        ],
        out_shape=[
            jax.ShapeDtypeStruct((b, s, MAIN_W), F32),
            jax.ShapeDtypeStruct((b, s, HEAD_W), F32),
        ],
        scratch_shapes=[pltpu.VMEM((tm, d), BF16)],
        compiler_params=_cparams(("parallel", "parallel", "arbitrary")),
        name="in_proj",
    )(x, nw, w_main, w_small, a_log_lanes, dt_bias_lanes)


def _attn_kernel(lq1_ref, lk1_ref, lq2_ref, lk2_ref, q_ref, k_ref, v_ref, cos_ref, sin_ref,
                 subw_ref, o_ref, kr_scr, vt_scr, q2_scr, m_scr, acc_scr, *chunk_scr, tq, seq):
    qi = pl.program_id(2)
    n_q = seq // tq
    kc = min(ATTN_KEY_CHUNK, seq)
    n_kc = seq // kc
    slab = 64
    s_scr = (chunk_scr[:n_kc], chunk_scr[n_kc:2 * n_kc])
    p_scr = chunk_scr[2 * n_kc:]
    lane = lax.broadcasted_iota(jnp.int32, (1, HEAD_W), 1)
    first_half = (lane % DA_DH) < (DA_DH // 2)
    lo = lane < DA_DH

    def rope(t, cos, sin_signed):
        rot = jnp.where(first_half, pltpu.roll(t, HEAD_W - DA_DH // 2, 1),
                        pltpu.roll(t, DA_DH // 2, 1))
        return t * cos + rot * sin_signed

    def stacked_queries(tile):
        rows = slice(tile * tq, (tile + 1) * tq)
        q = rope(q_ref[0, rows, :], cos_ref[rows, :], sin_ref[rows, :]) * (DA_DH ** -0.5)
        return jnp.concatenate([jnp.where(lo, q, 0.0), jnp.where(lo, 0.0, q)], axis=0).astype(BF16)

    def scores_and_max(q2, dst, c, m8):
        dst[...] = _dot_nt(kr_scr[c * kc:(c + 1) * kc, :], q2)
        for r0s in range(0, kc, slab):
            st = dst[r0s:r0s + slab, :]
            m8 = jnp.maximum(m8, jnp.max(st.reshape(slab // 8, 8, 2 * tq), axis=0))
        return m8

    def finish(tile):
        lam = (jnp.exp(jnp.sum(lq1_ref[...] * lk1_ref[...], axis=-1, keepdims=True))
               - jnp.exp(jnp.sum(lq2_ref[...] * lk2_ref[...], axis=-1, keepdims=True))
               + LAMBDA_INIT)
        inv_l = 1.0 / acc_scr[HEAD_W:HEAD_W + 1, :]
        o_t = (acc_scr[:HEAD_W, :tq] * inv_l[:, :tq]
               - lam * (acc_scr[:HEAD_W, tq:] * inv_l[:, tq:]))
        o = _rms(o_t.T, subw_ref[...]) * (1.0 - LAMBDA_INIT)
        o_ref[0, pl.ds(pl.multiple_of(tile * tq, tq), tq), :] = o.astype(o_ref.dtype)

    neg = jnp.full((8, 2 * tq), -1e30, F32)

    @pl.when(qi == 0)
    def _():
        kr_scr[...] = rope(k_ref[0], cos_ref[...], sin_ref[...]).astype(BF16)
        for c in range(n_kc):
            vt_scr[c, :HEAD_W, :] = v_ref[0, c * kc:(c + 1) * kc, :].T.astype(BF16)
            vt_scr[c, HEAD_W:, :] = jnp.ones((ONES_ROWS, kc), BF16)
        q2 = stacked_queries(0)
        q2_scr[0] = q2
        m8 = neg
        per_chunk = -(-(n_q - 1) // n_kc)
        for c in range(n_kc):
            m8 = scores_and_max(q2, s_scr[0][c], c, m8)
            for t in range(1 + c * per_chunk, min(n_q, 1 + (c + 1) * per_chunk)):
                q2_scr[t] = stacked_queries(t)
        m_scr[...] = m8
        acc_scr[...] = jnp.ones(acc_scr.shape, F32)

    def tile(parity):
        q2_next = q2_scr[(qi + 1) % n_q]
        m = jnp.max(m_scr[...], axis=0, keepdims=True)
        m8 = neg
        acc = jnp.zeros((HEAD_W + ONES_ROWS, 2 * tq), F32)
        for c in range(n_kc):
            m8 = scores_and_max(q2_next, s_scr[1 - parity][c], c, m8)
            for r0s in range(0, kc, slab):
                p = jnp.exp(s_scr[parity][c][r0s:r0s + slab, :] - m)
                p_scr[c][r0s:r0s + slab, :] = p.astype(BF16)
            acc = acc + _dot(vt_scr[c], p_scr[c][...])
            if c == 0:
                finish((qi + n_q - 1) % n_q)
        m_scr[...] = m8
        acc_scr[...] = acc

    for parity in range(2):
        pl.when(qi % 2 == parity)(functools.partial(tile, parity))

    @pl.when(qi == n_q - 1)
    def _():
        finish(qi)


def _attention(p_main, cos, sin_signed, lq1, lk1, lq2, lk2, subw):
    b, s, _ = p_main.shape
    tq = min(256, s // 2)
    kc = min(ATTN_KEY_CHUNK, s)
    n_kc = s // kc
    vec = pl.BlockSpec((1, DA_DH), lambda i, h, q: (0, 0))

    def col(blk):
        return pl.BlockSpec((1, s, HEAD_W), lambda i, h, q: (i, 0, blk + h))

    return pl.pallas_call(
        functools.partial(_attn_kernel, tq=tq, seq=s),
        grid=(b, N_HEADS, s // tq),
        in_specs=[
            vec, vec, vec, vec,
            col(QA_BLK), col(KA_BLK), col(VA_BLK),
            pl.BlockSpec((s, HEAD_W), lambda i, h, q: (0, 0)),
            pl.BlockSpec((s, HEAD_W), lambda i, h, q: (0, 0)),
            pl.BlockSpec((1, HEAD_W), lambda i, h, q: (0, 0)),
        ],
        out_specs=pl.BlockSpec((1, s, HEAD_W), lambda i, h, q: (i, 0, h)),
        out_shape=jax.ShapeDtypeStruct((b, s, N_HEADS * HEAD_W), BF16),
        scratch_shapes=([pltpu.VMEM((s, HEAD_W), BF16),
                         pltpu.VMEM((n_kc, HEAD_W + ONES_ROWS, kc), BF16),
                         pltpu.VMEM((s // tq, 2 * tq, HEAD_W), BF16),
                         pltpu.VMEM((8, 2 * tq), F32),
                         pltpu.VMEM((HEAD_W + ONES_ROWS, 2 * tq), F32)]
                        + [pltpu.VMEM((kc, 2 * tq), F32)] * (2 * n_kc)
                        + [pltpu.VMEM((kc, 2 * tq), BF16)] * n_kc),
        compiler_params=_cparams(("parallel", "parallel", "arbitrary")),
        name="diff_attn",
    )(lq1, lk1, lq2, lk2, p_main, p_main, p_main, cos, sin_signed, subw)


def _gdn_kernel(q_ref, k_ref, v_ref, z_ref, sm_ref, cwq_ref, cwk_ref, cwv_ref, nw_ref,
                o_ref,
                pad_scr, q_scr, k_scr, v_scr, sw_scr, su_scr, os_scr, ou_scr, gl_scr,
                o_scr, *, seq):
    C = GDN_CHUNK
    n_chunks = seq // C
    h = pl.program_id(1)
    halo = 8
    conv_rows = min(512, seq)

    zeros_halo = jnp.zeros((halo, HEAD_W), F32)
    pad_scr[0:halo, :] = zeros_halo
    pad_scr[halo + seq:2 * halo + seq, :] = zeros_halo

    def conv_silu(src_ref, cw_ref, dst_scr, normalize):
        pad_scr[halo:halo + seq, :] = src_ref[0]
        cw = cw_ref[...]
        for r0 in range(0, seq, conv_rows):
            acc = None
            for t in range(CONV_K):
                start = halo + r0 + t - CONV_K // 2
                term = cw[t:t + 1, :] * pad_scr[start:start + conv_rows, :]
                acc = term if acc is None else acc + term
            y = acc * jax.nn.sigmoid(acc)
            if normalize:
                y = y * lax.rsqrt(jnp.sum(y * y, axis=-1, keepdims=True) + L2_EPS)
            dst_scr[r0:r0 + conv_rows, :] = y

    conv_silu(q_ref, cwq_ref, q_scr, True)
    conv_silu(k_ref, cwk_ref, k_scr, True)
    conv_silu(v_ref, cwv_ref, v_scr, False)

    ri = lax.broadcasted_iota(jnp.int32, (C, C), 0)
    ci = lax.broadcasted_iota(jnp.int32, (C, C), 1)
    incl = (ci <= ri, ci >= ri)
    strict = (ci < ri, ci > ri)
    lane = lax.broadcasted_iota(jnp.int32, (1, HEAD_W), 1)
    dk_scale = HEAD_W ** -0.5

    def lane_pick(t, idx):
        return jnp.sum(jnp.where(lane == idx, t, 0.0), axis=-1, keepdims=True)

    def split_bf16(t, terms):
        parts = []
        for _ in range(terms - 1):
            p = t.astype(BF16)
            parts.append(p)
            t = t - p.astype(F32)
        parts.append(t.astype(BF16))
        return parts

    def bdot(a, b):
        return lax.dot_general(a, b, (((2,), (1,)), ((0,), (0,))), preferred_element_type=F32)

    def bdot_nt(a, b):
        return lax.dot_general(a, b, (((2,), (2,)), ((0,), (0,))), preferred_element_type=F32)

    def bdot_tn(a, b):
        return lax.dot_general(a, b, (((1,), (1,)), ((0,), (0,))), preferred_element_type=F32)

    nb = min(16, n_chunks)
    rows_b = nb * C
    incl16 = tuple(jnp.broadcast_to(m.astype(BF16), (nb, C, C)) for m in incl)
    LANE_GROUP_W = HEAD_W // 4
    lane_c = lax.broadcasted_iota(jnp.int32, (C, HEAD_W), 1)
    group_first_lane16 = jnp.broadcast_to(
        ((lane_c % LANE_GROUP_W == 0) & (lane_c < 3 * LANE_GROUP_W)).astype(BF16), (nb, C, HEAD_W))

    def by_lane_group(parts):
        p0, p1, p2 = (p.astype(F32) for p in parts)
        return jnp.where(lane_c < LANE_GROUP_W, p0,
                         jnp.where(lane_c < 2 * LANE_GROUP_W, p1,
                                   jnp.where(lane_c < 3 * LANE_GROUP_W, p2, 0.0))).astype(BF16)

    def precompute(blk, carry):
        rows = pl.ds(pl.multiple_of(blk * rows_b, rows_b), rows_b)
        q = (q_scr[rows, :] * dk_scale).reshape(nb, C, HEAD_W)
        k = k_scr[rows, :].reshape(nb, C, HEAD_W)
        v = v_scr[rows, :].reshape(nb, C, HEAD_W)
        sm = sm_ref[0, rows, :].reshape(nb, C, HEAD_W)
        k16 = k.astype(BF16)
        qk_raw = bdot_nt(q.astype(BF16), k16)
        dirs = range(2)
        beta, cum, decay, k_beta, a_mat = [], [], [], [], []
        for d in dirs:
            g = lane_pick(sm, d * N_HEADS + h)
            beta.append(lane_pick(sm, (2 + d) * N_HEADS + h))
            cum_terms = bdot(incl16[d], by_lane_group(split_bf16(jnp.broadcast_to(g, (nb, C, HEAD_W)), 3)))
            cum_d = cum_terms
            for shift in (LANE_GROUP_W, 2 * LANE_GROUP_W, 3 * LANE_GROUP_W):
                cum_d = cum_d + pltpu.roll(cum_terms, shift, 2)
            cum.append(cum_d)
            cum_i = cum_d[:, :, :C]
            cum_j = bdot_nt(group_first_lane16, by_lane_group(split_bf16(cum_d, 3)))
            decay.append(jnp.where(incl[d], jnp.exp(jnp.where(incl[d], cum_i - cum_j, 0.0)), 0.0))
            k_beta.append(k * beta[d])
            a_mat.append(jnp.where(strict[d], bdot_nt(k_beta[d].astype(BF16), k16) * decay[d], 0.0))

        a16 = [a_mat[d].astype(BF16) for d in dirs]
        pw = [bdot(a16[d], a16[d]) for d in dirs]
        r = [-a_mat[d] for d in dirs]
        for step in range(int(math.log2(C)) - 1):
            if step > 0:
                pw16 = [pw[d].astype(BF16) for d in dirs]
                pw = [bdot(pw16[d], pw16[d]) for d in dirs]
            r = [r[d] + pw[d] + bdot(r[d].astype(BF16), pw[d].astype(BF16)) for d in dirs]

        for d in dirs:
            r16 = r[d].astype(BF16)
            e_cum = jnp.exp(cum[d])
            v_beta = v * beta[d]
            kb_dec = k_beta[d] * e_cum
            r_rhs = bdot(r16, jnp.concatenate([v_beta.astype(BF16), kb_dec.astype(BF16)], axis=-1))
            u = v_beta + r_rhs[:, :, :HEAD_W]
            w = kb_dec + r_rhs[:, :, HEAD_W:]
            cum_end = cum[d][:, C - 1:C, :] if d == 0 else cum[d][:, 0:1, :]
            wu16 = jnp.concatenate([w.astype(BF16), u.astype(BF16)], axis=-1)
            kd16 = (k * jnp.exp(cum_end - cum[d])).astype(BF16)
            qk16 = jnp.where(incl[d], qk_raw * decay[d], 0.0).astype(BF16)
            kd_wu = bdot_tn(kd16, wu16)
            qk_wu = bdot(qk16, wu16)
            rows_s = pl.ds(pl.multiple_of(blk * (nb * HEAD_W), nb * HEAD_W), nb * HEAD_W)
            sw_scr[d, rows_s, :] = kd_wu[:, :, :HEAD_W].astype(BF16).reshape(nb * HEAD_W, HEAD_W)
            su_scr[d, rows_s, :] = kd_wu[:, :, HEAD_W:].astype(BF16).reshape(nb * HEAD_W, HEAD_W)
            os_scr[d, rows, :] = (q * e_cum - qk_wu[:, :, :HEAD_W]).astype(BF16).reshape(rows_b, HEAD_W)
            ou_scr[d, rows, :] = qk_wu[:, :, HEAD_W:].reshape(rows_b, HEAD_W)
            gl_scr[d, pl.ds(pl.multiple_of(blk * (nb * 8), nb * 8), nb * 8), :] = jnp.broadcast_to(
                jnp.exp(cum_end), (nb, 8, HEAD_W)).reshape(nb * 8, HEAD_W)
        return carry

    lax.fori_loop(0, n_chunks // nb, precompute, 0)

    def scan_step(t, states):
        new_states = []
        for d in range(2):
            n = t if d == 0 else n_chunks - 1 - t
            rows = pl.ds(pl.multiple_of(n * C, C), C)
            rows_s = pl.ds(pl.multiple_of(n * HEAD_W, HEAD_W), HEAD_W)
            st = states[d]
            st16 = st.astype(BF16)
            o_scr[d, rows, :] = ou_scr[d, rows, :] + _dot(os_scr[d, rows, :], st16)
            gl = gl_scr[d, pl.ds(pl.multiple_of(n * 8, 8), 1), :]
            new_states.append(st * gl + su_scr[d, rows_s, :].astype(F32)
                              - _dot(sw_scr[d, rows_s, :], st16))
        return tuple(new_states)

    zero_state = jnp.zeros((HEAD_W, HEAD_W), F32)
    lax.fori_loop(0, n_chunks, scan_step, (zero_state, zero_state))

    for r0 in range(0, seq, conv_rows):
        sl = slice(r0, r0 + conv_rows)
        o = _rms(o_scr[0, sl, :] + o_scr[1, sl, :], nw_ref[...])
        z = z_ref[0, sl, :]
        o_ref[0, sl, :] = (o * (z * jax.nn.sigmoid(z))).astype(o_ref.dtype)


def _gdn(p_main, gates, conv_w, nw):
    b, s, _ = p_main.shape

    def col(blk):
        return pl.BlockSpec((1, s, HEAD_W), lambda i, h: (i, 0, blk + h))

    def cw(blk):
        return pl.BlockSpec((CONV_K, HEAD_W), lambda i, h: (0, blk + h))

    return pl.pallas_call(
        functools.partial(_gdn_kernel, seq=s),
        grid=(b, N_HEADS),
        in_specs=[
            col(GQ_BLK), col(GK_BLK), col(GV_BLK), col(Z_BLK),
            pl.BlockSpec((1, s, HEAD_W), lambda i, h: (i, 0, 0)),
            cw(0), cw(N_HEADS), cw(2 * N_HEADS),
            pl.BlockSpec((1, HEAD_W), lambda i, h: (0, 0)),
        ],
        out_specs=pl.BlockSpec((1, s, HEAD_W), lambda i, h: (i, 0, h)),
        out_shape=jax.ShapeDtypeStruct((b, s, N_HEADS * HEAD_W), BF16),
        scratch_shapes=[
            pltpu.VMEM((s + 16, HEAD_W), F32),
            pltpu.VMEM((s, HEAD_W), F32),
            pltpu.VMEM((s, HEAD_W), F32),
            pltpu.VMEM((s, HEAD_W), F32),
            pltpu.VMEM((2, (s // GDN_CHUNK) * HEAD_W, HEAD_W), BF16),
            pltpu.VMEM((2, (s // GDN_CHUNK) * HEAD_W, HEAD_W), BF16),
            pltpu.VMEM((2, s, HEAD_W), BF16),
            pltpu.VMEM((2, s, HEAD_W), F32),
            pltpu.VMEM((2, (s // GDN_CHUNK) * 8, HEAD_W), F32),
            pltpu.VMEM((2, s, HEAD_W), F32),
        ],
        compiler_params=_cparams(("parallel", "arbitrary"), vmem=VMEM_LIMIT_LARGE),
        name="gdn",
    )(p_main, p_main, p_main, p_main, gates, conv_w, conv_w, conv_w, nw)


def _merge_kernel(x_ref, oa_ref, og_ref, ga_ref, gg_ref, wpa_ref, wpg_ref, wo_ref, n2w_ref,
                  wr_ref, x2_ref, aff_ref):
    ya = _dot(oa_ref[0], wpa_ref[...])
    yg = _dot(og_ref[0], wpg_ref[...])
    merged = jax.nn.sigmoid(ga_ref[0]) * ya + jax.nn.sigmoid(gg_ref[0]) * yg
    x2 = x_ref[0] + _dot(merged.astype(BF16), wo_ref[...])
    tm = x2.shape[0]
    for j in range(LANE_GROUPS):
        x2_ref[0, pl.ds(j, tm, stride=LANE_GROUPS), :] = x2[:, j * HEAD_W:(j + 1) * HEAD_W]
    h2 = _rms(x2, n2w_ref[...])
    wr = wr_ref[...]
    h_hi = h2.astype(BF16)
    h_lo = (h2 - h_hi.astype(F32)).astype(BF16)
    w_hi = wr.astype(BF16)
    w_lo = (wr - w_hi.astype(F32)).astype(BF16)
    logits = _dot(h_hi, w_hi) + (_dot(h_hi, w_lo) + _dot(h_lo, w_hi))
    lane = lax.broadcasted_iota(jnp.int32, logits.shape, 1)
    valid = lane < N_EXPERTS
    logits = jnp.where(valid, logits, -1e30)
    e = jnp.exp(logits - jnp.max(logits, axis=-1, keepdims=True))
    e = jnp.where(valid, e, 0.0)
    aff_ref[0] = e * (1.0 / jnp.sum(e, axis=-1, keepdims=True))


def _merge(x, oa, og, p_main, wpa, wpg, wo, n2w, wr):
    b, s, d = x.shape
    tm = min(512, s)
    tok = lambda i, m: (i, m, 0)
    const = lambda i, m: (0, 0)
    return pl.pallas_call(
        _merge_kernel,
        grid=(b, s // tm),
        in_specs=[
            pl.BlockSpec((1, tm, d), tok),
            pl.BlockSpec((1, tm, d), tok),
            pl.BlockSpec((1, tm, d), tok),
            pl.BlockSpec((1, tm, d), lambda i, m: (i, m, GATE_A_BLK1024)),
            pl.BlockSpec((1, tm, d), lambda i, m: (i, m, GATE_G_BLK1024)),
            pl.BlockSpec((d, d), const),
            pl.BlockSpec((d, d), const),
            pl.BlockSpec((d, d), const),
            pl.BlockSpec((1, d), const),
            pl.BlockSpec((d, HEAD_W), const),
        ],
        out_specs=[pl.BlockSpec((1, tm * LANE_GROUPS, HEAD_W), tok),
                   pl.BlockSpec((1, tm, HEAD_W), tok)],
        out_shape=[jax.ShapeDtypeStruct((b, s * LANE_GROUPS, HEAD_W), F32),
                   jax.ShapeDtypeStruct((b, s, HEAD_W), F32)],
        compiler_params=_cparams(("parallel", "parallel")),
        name="merge",
    )(x, oa, og, p_main, p_main, wpa, wpg, wo, n2w, wr)


def _route_kernel(aff_ref, idx_ref, val_ref, csm_scr, *, seq, cap):
    blk = 128
    bits = lax.bitcast_convert_type(aff_ref[0], jnp.int32)

    groups = HEAD_W // N_EXPERTS
    packed = aff_ref[0, pl.ds(0, seq // groups, stride=groups), :]
    for g in range(1, groups):
        packed = packed + pltpu.roll(aff_ref[0, pl.ds(g, seq // groups, stride=groups), :],
                                     g * N_EXPERTS, 1)
    bits_packed = lax.bitcast_convert_type(packed, jnp.int32)

    def over_groups(cnt):
        shift = N_EXPERTS
        while shift < HEAD_W:
            cnt = cnt + pltpu.roll(cnt, shift, 1)
            shift *= 2
        return cnt

    def bisect(i, cur):
        cand = cur | jnp.left_shift(jnp.int32(1), 30 - i)
        cnt = over_groups(jnp.sum((bits_packed >= cand).astype(F32), axis=0, keepdims=True))
        return jnp.where(cnt >= cap, cand, cur)

    thr = lax.fori_loop(0, 31, bisect, jnp.zeros((1, HEAD_W), jnp.int32))
    n_gt = jnp.sum((bits > thr).astype(F32), axis=0, keepdims=True)
    n_ties = cap - n_gt

    ri = lax.broadcasted_iota(jnp.int32, (blk, blk), 0)
    ci = lax.broadcasted_iota(jnp.int32, (blk, blk), 1)
    ltri = (ci <= ri).astype(BF16)

    def select(j, carry):
        tie_base, sel_base = carry
        rows = pl.ds(pl.multiple_of(j * blk, blk), blk)
        bj = lax.bitcast_convert_type(aff_ref[0, rows, :], jnp.int32)
        eq = jnp.where(bj == thr, 1.0, 0.0)
        tie_rank = _dot(ltri, eq.astype(BF16)) + tie_base - eq
        sel = jnp.where(bj > thr, 1.0, eq * jnp.where(tie_rank < n_ties, 1.0, 0.0))
        cs = _dot(ltri, sel.astype(BF16)) + sel_base
        csm_scr[rows, :] = cs * sel
        return (tie_base + jnp.sum(eq, axis=0, keepdims=True),
                sel_base + jnp.sum(sel, axis=0, keepdims=True))

    zero = jnp.zeros((1, HEAD_W), F32)
    lax.fori_loop(0, seq // blk, select, (zero, zero))

    rb = min(512, seq)
    slot = lax.broadcasted_iota(jnp.int32, (1, cap), 1).astype(F32) + 1.0
    for e in range(N_EXPERTS):
        def gather_slots(j, carry):
            idx_acc, val_acc = carry
            r0 = pl.multiple_of(j * rb, rb)
            rows = pl.ds(r0, rb)
            onehot = csm_scr[rows, e:e + 1] == slot
            tok = (lax.broadcasted_iota(jnp.int32, (rb, 1), 0) + r0).astype(F32)
            idx_acc = idx_acc + jnp.sum(jnp.where(onehot, tok, 0.0), axis=0, keepdims=True)
            val_acc = val_acc + jnp.sum(jnp.where(onehot, aff_ref[0, rows, e:e + 1], 0.0),
                                        axis=0, keepdims=True)
            return idx_acc, val_acc

        zc = jnp.zeros((1, cap), F32)
        idx_e, val_e = lax.fori_loop(0, seq // rb, gather_slots, (zc, zc))
        idx_ref[0, e:e + 1, :] = idx_e.astype(jnp.int32)
        val_ref[0, e:e + 1, :] = val_e


def _route(aff, cap):
    b, s, _ = aff.shape
    return pl.pallas_call(
        functools.partial(_route_kernel, seq=s, cap=cap),
        grid=(b,),
        in_specs=[pl.BlockSpec((1, s, HEAD_W), lambda i: (i, 0, 0))],
        out_specs=[pl.BlockSpec((1, N_EXPERTS, cap), lambda i: (i, 0, 0)),
                   pl.BlockSpec((1, N_EXPERTS, cap), lambda i: (i, 0, 0))],
        out_shape=[jax.ShapeDtypeStruct((b, N_EXPERTS, cap), jnp.int32),
                   jax.ShapeDtypeStruct((b, N_EXPERTS, cap), F32)],
        scratch_shapes=[pltpu.VMEM((s, HEAD_W), F32)],
        compiler_params=_cparams(("parallel",)),
        name="route",
    )(aff)


def _moe_kernel(idx_ref, val_ref, x2_hbm, n2w_ref, wg_ref, wu_ref, wd_ref, out_hbm,
                x2_v, acc_v, xg_v, xs_v, y_v, yt_v, in_sem, acc_sem, out_sem, *, cap, n_ff):
    G = LANE_GROUPS
    b = pl.program_id(0)
    e = pl.program_id(1)
    f = pl.program_id(2)
    slot0 = (b * N_EXPERTS + e) * cap

    def tile_rows(t):
        return pl.ds(pl.multiple_of(t * G, G), G)

    cp_x = pltpu.make_async_copy(x2_hbm.at[b], x2_v, in_sem)
    cp_a = pltpu.make_async_copy(x2_hbm.at[b], acc_v, acc_sem)

    @pl.when((e == 0) & (f == 0))
    def _():
        cp_x.start()
        cp_a.start()
        cp_x.wait()

    @pl.when(f == 0)
    def _():
        def gather(c, carry):
            xg_v[tile_rows(c), :] = x2_v[tile_rows(idx_ref[slot0 + c]), :]
            return carry

        lax.fori_loop(0, cap, gather, 0, unroll=8)
        groups = [xg_v[pl.ds(j, cap, stride=G), :] for j in range(G)]
        ssq = sum(jnp.sum(g * g, axis=-1, keepdims=True) for g in groups)
        scale = lax.rsqrt(ssq * (1.0 / D_MODEL) + NORM_EPS)
        for j in range(G):
            cols = slice(j * HEAD_W, (j + 1) * HEAD_W)
            xs_v[:, cols] = (groups[j] * scale * n2w_ref[:, cols]).astype(BF16)

    xs = xs_v[...]
    gate = _dot(xs, wg_ref[0])
    up = _dot(xs, wu_ref[0])
    act = (gate * jax.nn.sigmoid(gate) * up).astype(BF16)
    y_part = _dot(act, wd_ref[0])

    @pl.when(f == 0)
    def _():
        y_v[...] = y_part

    @pl.when(f > 0)
    def _():
        y_v[...] += y_part

    @pl.when(f == n_ff - 1)
    def _():
        for j in range(G):
            yt_v[pl.ds(j, cap, stride=G), :] = y_v[:, j * HEAD_W:(j + 1) * HEAD_W]

        @pl.when(e == 0)
        def _():
            cp_a.wait()

        def scatter(g, carry):
            c0 = g * SCATTER_BATCH
            dsts = [tile_rows(idx_ref[slot0 + c0 + j]) for j in range(SCATTER_BATCH)]
            new = [acc_v[dsts[j], :] + yt_v[tile_rows(c0 + j), :] * val_ref[slot0 + c0 + j]
                   for j in range(SCATTER_BATCH)]
            for j in range(SCATTER_BATCH):
                acc_v[dsts[j], :] = new[j]
            return carry

        lax.fori_loop(0, cap // SCATTER_BATCH, scatter, 0)

    @pl.when((e == N_EXPERTS - 1) & (f == n_ff - 1))
    def _():
        cp = pltpu.make_async_copy(acc_v, out_hbm.at[b], out_sem)
        cp.start()
        cp.wait()


def _moe(x2t, idx, val, n2w, wg, wu, wd):
    b, rows, _ = x2t.shape
    d = D_MODEL
    cap = idx.shape[-1]
    tf = 1024
    n_ff = EXPERT_FF // tf
    grid_spec = pltpu.PrefetchScalarGridSpec(
        num_scalar_prefetch=2,
        grid=(b, N_EXPERTS, n_ff),
        in_specs=[
            pl.BlockSpec(memory_space=pl.ANY),
            pl.BlockSpec((1, d), lambda i, e, f, *_: (0, 0)),
            pl.BlockSpec((1, d, tf), lambda i, e, f, *_: (e, 0, f)),
            pl.BlockSpec((1, d, tf), lambda i, e, f, *_: (e, 0, f)),
            pl.BlockSpec((1, tf, d), lambda i, e, f, *_: (e, f, 0)),
        ],
        out_specs=pl.BlockSpec(memory_space=pl.ANY),
        scratch_shapes=[
            pltpu.VMEM((rows, HEAD_W), F32),
            pltpu.VMEM((rows, HEAD_W), F32),
            pltpu.VMEM((cap * LANE_GROUPS, HEAD_W), F32),
            pltpu.VMEM((cap, d), BF16),
            pltpu.VMEM((cap, d), F32),
            pltpu.VMEM((cap * LANE_GROUPS, HEAD_W), F32),
            pltpu.SemaphoreType.DMA,
            pltpu.SemaphoreType.DMA,
            pltpu.SemaphoreType.DMA,
        ],
    )
    return pl.pallas_call(
        functools.partial(_moe_kernel, cap=cap, n_ff=n_ff),
        grid_spec=grid_spec,
        out_shape=jax.ShapeDtypeStruct(x2t.shape, F32),
        compiler_params=_cparams(("arbitrary", "arbitrary", "arbitrary"), vmem=VMEM_LIMIT_LARGE),
        name="moe",
    )(idx.reshape(-1), val.reshape(-1), x2t, n2w, wg, wu, wd)


def _final_norm_kernel(xt_ref, w_ref, o_ref):
    tm = o_ref.shape[1]
    groups = [xt_ref[0, pl.ds(j, tm, stride=LANE_GROUPS), :] for j in range(LANE_GROUPS)]
    ssq = sum(jnp.sum(g * g, axis=-1, keepdims=True) for g in groups)
    scale = lax.rsqrt(ssq * (1.0 / D_MODEL) + NORM_EPS)
    for j in range(LANE_GROUPS):
        cols = slice(j * HEAD_W, (j + 1) * HEAD_W)
        o_ref[0, :, cols] = groups[j] * scale * w_ref[:, cols]


def _final_norm(xt, w):
    b, rows, _ = xt.shape
    s = rows // LANE_GROUPS
    tm = min(512, s)
    return pl.pallas_call(
        _final_norm_kernel,
        grid=(b, s // tm),
        in_specs=[pl.BlockSpec((1, tm * LANE_GROUPS, HEAD_W), lambda i, m: (i, m, 0)),
                  pl.BlockSpec((1, D_MODEL), lambda i, m: (0, 0))],
        out_specs=pl.BlockSpec((1, tm, D_MODEL), lambda i, m: (i, m, 0)),
        out_shape=jax.ShapeDtypeStruct((b, s, D_MODEL), F32),
        compiler_params=_cparams(("parallel", "parallel")),
        name="final_norm",
    )(xt, w)


def _rope_tables(seq):
    inv = ROPE_THETA ** (-jnp.arange(0, DA_DH, 2, dtype=F32) / DA_DH)
    ang = jnp.arange(seq, dtype=F32)[:, None] * inv[None, :]
    cos, sin = jnp.cos(ang), jnp.sin(ang)
    return (jnp.concatenate([cos, cos, cos, cos], axis=-1),
            jnp.concatenate([-sin, sin, -sin, sin], axis=-1))


def kernel(x, norm1_w, w_in, conv_w, a_log_fwd, dt_bias_fwd, a_log_bwd, dt_bias_bwd, gdn_norm_w,
           lambda_q1, lambda_k1, lambda_q2, lambda_k2, subln_w, w_proj_attn, w_proj_gdn, w_out,
           norm2_w, w_router, w_gate, w_up, w_down, norm_f_w):
    assert w_in.shape[0] == 1, "single-layer block"
    b, s, d = x.shape
    cap = CAP_FACTOR * s // N_EXPERTS
    w = w_in[0]
    small0 = 7 * D_MODEL
    small_w = 4 * N_HEADS
    w_main = jnp.concatenate([w[:, :small0], w[:, small0 + small_w:]], axis=1).astype(BF16)
    w_small = jnp.pad(w[:, small0:small0 + small_w], ((0, 0), (0, HEAD_W - small_w))).astype(BF16)

    pad_lanes = jnp.zeros((HEAD_W - 2 * N_HEADS,), F32)
    a_log_lanes = jnp.concatenate([a_log_fwd[0], a_log_bwd[0], pad_lanes]).reshape(1, HEAD_W)
    dt_bias_lanes = jnp.concatenate([dt_bias_fwd[0], dt_bias_bwd[0], pad_lanes]).reshape(1, HEAD_W)
    p_main, gates = _in_proj(x, norm1_w, w_main, w_small, a_log_lanes, dt_bias_lanes)

    cos, sin_signed = _rope_tables(s)
    oa = _attention(p_main, cos, sin_signed, lambda_q1, lambda_k1, lambda_q2, lambda_k2, subln_w)
    og = _gdn(p_main, gates, conv_w[0], gdn_norm_w)

    w_r = jnp.pad(w_router[0], ((0, 0), (0, HEAD_W - N_EXPERTS)))
    x2, aff = _merge(x, oa, og, p_main, w_proj_attn[0].astype(BF16), w_proj_gdn[0].astype(BF16),
                     w_out[0].astype(BF16), norm2_w, w_r)
    idx, val = _route(aff, cap)
    y = _moe(x2, idx, val, norm2_w, w_gate[0].astype(BF16), w_up[0].astype(BF16),
             w_down[0].astype(BF16))
    return _final_norm(y, norm_f_w.reshape(1, d))
```

```python
import functools
import math

import jax
import jax.numpy as jnp
from jax import lax
from jax.experimental import pallas as pl
from jax.experimental.pallas import tpu as pltpu

F32 = jnp.float32
BF16 = jnp.bfloat16

D_MODEL = 1024
N_HEADS = 8
HEAD_W = 128
LANE_GROUPS = D_MODEL // HEAD_W
DA_DH = 64
ROPE_THETA = 10000.0
CONV_K = 5
GDN_CHUNK = 64
ATTN_KEY_CHUNK = 256
ONES_ROWS = 16
SCATTER_BATCH = 16
N_EXPERTS = 16
EXPERT_FF = 2048
CAP_FACTOR = 2
NORM_EPS = 1e-6
L2_EPS = 1e-6
LAMBDA_INIT = 0.8 - 0.6 * math.exp(-0.3 * 0)

QA_BLK, KA_BLK, VA_BLK = 0, 8, 16
GQ_BLK, GK_BLK, GV_BLK, Z_BLK = 24, 32, 40, 48
GATE_A_BLK1024, GATE_G_BLK1024 = 7, 8
MAIN_W = 9 * D_MODEL

VMEM_LIMIT = 56 * 1024 * 1024
VMEM_LIMIT_LARGE = 60 * 1024 * 1024


def _cparams(sem, vmem=VMEM_LIMIT):
    return pltpu.CompilerParams(dimension_semantics=sem, vmem_limit_bytes=vmem)


def _dot(a, b, precision=None):
    return jnp.dot(a, b, preferred_element_type=F32, precision=precision)


def _dot_nt(a, b, precision=None):
    return lax.dot_general(a, b, (((1,), (1,)), ((), ())), preferred_element_type=F32,
                           precision=precision)


def _dot_tn(a, b, precision=None):
    return lax.dot_general(a, b, (((0,), (0,)), ((), ())), preferred_element_type=F32,
                           precision=precision)


def _rms(t, w, eps=NORM_EPS):
    return t * lax.rsqrt(jnp.mean(t * t, axis=-1, keepdims=True) + eps) * w


def _softplus(x):
    return jnp.maximum(x, 0.0) + jnp.log1p(jnp.exp(-jnp.abs(x)))


def _inproj_kernel(x_ref, nw_ref, wm_ref, ws_ref, alog_ref, dtb_ref, om_ref, os_ref, h_scr):
    @pl.when(pl.program_id(2) == 0)
    def _():
        h = _rms(x_ref[0], nw_ref[...]).astype(BF16)
        h_scr[...] = h
        logit = _dot(h, ws_ref[...])
        lane = lax.broadcasted_iota(jnp.int32, logit.shape, 1)
        g = -jnp.exp(alog_ref[...]) * _softplus(logit + dtb_ref[...])
        os_ref[0] = jnp.where(lane < 2 * N_HEADS, g, jax.nn.sigmoid(logit))

    om_ref[0] = _dot(h_scr[...], wm_ref[...])


def _in_proj(x, nw, w_main, w_small, a_log_lanes, dt_bias_lanes):
    b, s, d = x.shape
    tm = min(1024, s)
    tn = 1024
    return pl.pallas_call(
        _inproj_kernel,
        grid=(b, s // tm, MAIN_W // tn),
        in_specs=[
            pl.BlockSpec((1, tm, d), lambda i, m, n: (i, m, 0)),
            pl.BlockSpec((1, d), lambda i, m, n: (0, 0)),
            pl.BlockSpec((d, tn), lambda i, m, n: (0, n)),
            pl.BlockSpec((d, HEAD_W), lambda i, m, n: (0, 0)),
            pl.BlockSpec((1, HEAD_W), lambda i, m, n: (0, 0)),
            pl.BlockSpec((1, HEAD_W), lambda i, m, n: (0, 0)),
        ],
        out_specs=[
            pl.BlockSpec((1, tm, tn), lambda i, m, n: (i, m, n)),
            pl.BlockSpec((1, tm, HEAD_W), lambda i, m, n: (i, m, 0)),
        ],
        out_shape=[
            jax.ShapeDtypeStruct((b, s, MAIN_W), F32),
            jax.ShapeDtypeStruct((b, s, HEAD_W), F32),
        ],
        scratch_shapes=[pltpu.VMEM((tm, d), BF16)],
        compiler_params=_cparams(("parallel", "parallel", "arbitrary")),
        name="in_proj",
    )(x, nw, w_main, w_small, a_log_lanes, dt_bias_lanes)


def _attn_kernel(lq1_ref, lk1_ref, lq2_ref, lk2_ref, q_ref, k_ref, v_ref, cos_ref, sin_ref,
                 subw_ref, o_ref, kr_scr, vt_scr, q2_scr, m_scr, acc_scr, *chunk_scr, tq, seq):
    qi = pl.program_id(2)
    n_q = seq // tq
    kc = min(ATTN_KEY_CHUNK, seq)
    n_kc = seq // kc
    slab = 64
    s_scr = (chunk_scr[:n_kc], chunk_scr[n_kc:2 * n_kc])
    p_scr = chunk_scr[2 * n_kc:]
    lane = lax.broadcasted_iota(jnp.int32, (1, HEAD_W), 1)
    first_half = (lane % DA_DH) < (DA_DH // 2)
    lo = lane < DA_DH

    def rope(t, cos, sin_signed):
        rot = jnp.where(first_half, pltpu.roll(t, HEAD_W - DA_DH // 2, 1),
                        pltpu.roll(t, DA_DH // 2, 1))
        return t * cos + rot * sin_signed

    def stacked_queries(tile):
        rows = slice(tile * tq, (tile + 1) * tq)
        q = rope(q_ref[0, rows, :], cos_ref[rows, :], sin_ref[rows, :]) * (DA_DH ** -0.5)
        return jnp.concatenate([jnp.where(lo, q, 0.0), jnp.where(lo, 0.0, q)], axis=0).astype(BF16)

    def scores_and_max(q2, dst, c, m8):
        dst[...] = _dot_nt(kr_scr[c * kc:(c + 1) * kc, :], q2)
        for r0s in range(0, kc, slab):
            st = dst[r0s:r0s + slab, :]
            m8 = jnp.maximum(m8, jnp.max(st.reshape(slab // 8, 8, 2 * tq), axis=0))
        return m8

    def finish(tile):
        lam = (jnp.exp(jnp.sum(lq1_ref[...] * lk1_ref[...], axis=-1, keepdims=True))
               - jnp.exp(jnp.sum(lq2_ref[...] * lk2_ref[...], axis=-1, keepdims=True))
               + LAMBDA_INIT)
        inv_l = 1.0 / acc_scr[HEAD_W:HEAD_W + 1, :]
        o_t = (acc_scr[:HEAD_W, :tq] * inv_l[:, :tq]
               - lam * (acc_scr[:HEAD_W, tq:] * inv_l[:, tq:]))
        o = _rms(o_t.T, subw_ref[...]) * (1.0 - LAMBDA_INIT)
        o_ref[0, pl.ds(pl.multiple_of(tile * tq, tq), tq), :] = o.astype(o_ref.dtype)

    neg = jnp.full((8, 2 * tq), -1e30, F32)

    @pl.when(qi == 0)
    def _():
        kr_scr[...] = rope(k_ref[0], cos_ref[...], sin_ref[...]).astype(BF16)
        for c in range(n_kc):
            vt_scr[c, :HEAD_W, :] = v_ref[0, c * kc:(c + 1) * kc, :].T.astype(BF16)
            vt_scr[c, HEAD_W:, :] = jnp.ones((ONES_ROWS, kc), BF16)
        q2 = stacked_queries(0)
        q2_scr[0] = q2
        m8 = neg
        per_chunk = -(-(n_q - 1) // n_kc)
        for c in range(n_kc):
            m8 = scores_and_max(q2, s_scr[0][c], c, m8)
            for t in range(1 + c * per_chunk, min(n_q, 1 + (c + 1) * per_chunk)):
                q2_scr[t] = stacked_queries(t)
        m_scr[...] = m8
        acc_scr[...] = jnp.ones(acc_scr.shape, F32)

    def tile(parity):
        q2_next = q2_scr[(qi + 1) % n_q]
        m = jnp.max(m_scr[...], axis=0, keepdims=True)
        m8 = neg
        acc = jnp.zeros((HEAD_W + ONES_ROWS, 2 * tq), F32)
        for c in range(n_kc):
            m8 = scores_and_max(q2_next, s_scr[1 - parity][c], c, m8)
            for r0s in range(0, kc, slab):
                p = jnp.exp(s_scr[parity][c][r0s:r0s + slab, :] - m)
                p_scr[c][r0s:r0s + slab, :] = p.astype(BF16)
            acc = acc + _dot(vt_scr[c], p_scr[c][...])
            if c == 0:
                finish((qi + n_q - 1) % n_q)
        m_scr[...] = m8
        acc_scr[...] = acc

    for parity in range(2):
        pl.when(qi % 2 == parity)(functools.partial(tile, parity))

    @pl.when(qi == n_q - 1)
    def _():
        finish(qi)


def _attention(p_main, cos, sin_signed, lq1, lk1, lq2, lk2, subw):
    b, s, _ = p_main.shape
    tq = min(256, s // 2)
    kc = min(ATTN_KEY_CHUNK, s)
    n_kc = s // kc
    vec = pl.BlockSpec((1, DA_DH), lambda i, h, q: (0, 0))

    def col(blk):
        return pl.BlockSpec((1, s, HEAD_W), lambda i, h, q: (i, 0, blk + h))

    return pl.pallas_call(
        functools.partial(_attn_kernel, tq=tq, seq=s),
        grid=(b, N_HEADS, s // tq),
        in_specs=[
            vec, vec, vec, vec,
            col(QA_BLK), col(KA_BLK), col(VA_BLK),
            pl.BlockSpec((s, HEAD_W), lambda i, h, q: (0, 0)),
            pl.BlockSpec((s, HEAD_W), lambda i, h, q: (0, 0)),
            pl.BlockSpec((1, HEAD_W), lambda i, h, q: (0, 0)),
        ],
        out_specs=pl.BlockSpec((1, s, HEAD_W), lambda i, h, q: (i, 0, h)),
        out_shape=jax.ShapeDtypeStruct((b, s, N_HEADS * HEAD_W), BF16),
        scratch_shapes=([pltpu.VMEM((s, HEAD_W), BF16),
                         pltpu.VMEM((n_kc, HEAD_W + ONES_ROWS, kc), BF16),
                         pltpu.VMEM((s // tq, 2 * tq, HEAD_W), BF16),
                         pltpu.VMEM((8, 2 * tq), F32),
                         pltpu.VMEM((HEAD_W + ONES_ROWS, 2 * tq), F32)]
                        + [pltpu.VMEM((kc, 2 * tq), F32)] * (2 * n_kc)
                        + [pltpu.VMEM((kc, 2 * tq), BF16)] * n_kc),
        compiler_params=_cparams(("parallel", "parallel", "arbitrary")),
        name="diff_attn",
    )(lq1, lk1, lq2, lk2, p_main, p_main, p_main, cos, sin_signed, subw)


def _gdn_kernel(q_ref, k_ref, v_ref, z_ref, sm_ref, cwq_ref, cwk_ref, cwv_ref, nw_ref,
                o_ref,
                pad_scr, q_scr, k_scr, v_scr, sw_scr, su_scr, os_scr, ou_scr, gl_scr,
                o_scr, *, seq):
    C = GDN_CHUNK
    n_chunks = seq // C
    h = pl.program_id(1)
    halo = 8
    conv_rows = min(512, seq)

    zeros_halo = jnp.zeros((halo, HEAD_W), F32)
    pad_scr[0:halo, :] = zeros_halo
    pad_scr[halo + seq:2 * halo + seq, :] = zeros_halo

    def conv_silu(src_ref, cw_ref, dst_scr, normalize):
        pad_scr[halo:halo + seq, :] = src_ref[0]
        cw = cw_ref[...]
        for r0 in range(0, seq, conv_rows):
            acc = None
            for t in range(CONV_K):
                start = halo + r0 + t - CONV_K // 2
                term = cw[t:t + 1, :] * pad_scr[start:start + conv_rows, :]
                acc = term if acc is None else acc + term
            y = acc * jax.nn.sigmoid(acc)
            if normalize:
                y = y * lax.rsqrt(jnp.sum(y * y, axis=-1, keepdims=True) + L2_EPS)
            dst_scr[r0:r0 + conv_rows, :] = y

    conv_silu(q_ref, cwq_ref, q_scr, True)
    conv_silu(k_ref, cwk_ref, k_scr, True)
    conv_silu(v_ref, cwv_ref, v_scr, False)

    ri = lax.broadcasted_iota(jnp.int32, (C, C), 0)
    ci = lax.broadcasted_iota(jnp.int32, (C, C), 1)
    incl = (ci <= ri, ci >= ri)
    strict = (ci < ri, ci > ri)
    lane = lax.broadcasted_iota(jnp.int32, (1, HEAD_W), 1)
    dk_scale = HEAD_W ** -0.5

    def lane_pick(t, idx):
        return jnp.sum(jnp.where(lane == idx, t, 0.0), axis=-1, keepdims=True)

    def split_bf16(t, terms):
        parts = []
        for _ in range(terms - 1):
            p = t.astype(BF16)
            parts.append(p)
            t = t - p.astype(F32)
        parts.append(t.astype(BF16))
        return parts

    def bdot(a, b):
        return lax.dot_general(a, b, (((2,), (1,)), ((0,), (0,))), preferred_element_type=F32)

    def bdot_nt(a, b):
        return lax.dot_general(a, b, (((2,), (2,)), ((0,), (0,))), preferred_element_type=F32)

    def bdot_tn(a, b):
        return lax.dot_general(a, b, (((1,), (1,)), ((0,), (0,))), preferred_element_type=F32)

    nb = min(16, n_chunks)
    rows_b = nb * C
    incl16 = tuple(jnp.broadcast_to(m.astype(BF16), (nb, C, C)) for m in incl)
    LANE_GROUP_W = HEAD_W // 4
    lane_c = lax.broadcasted_iota(jnp.int32, (C, HEAD_W), 1)
    group_first_lane16 = jnp.broadcast_to(
        ((lane_c % LANE_GROUP_W == 0) & (lane_c < 3 * LANE_GROUP_W)).astype(BF16), (nb, C, HEAD_W))

    def by_lane_group(parts):
        p0, p1, p2 = (p.astype(F32) for p in parts)
        return jnp.where(lane_c < LANE_GROUP_W, p0,
                         jnp.where(lane_c < 2 * LANE_GROUP_W, p1,
                                   jnp.where(lane_c < 3 * LANE_GROUP_W, p2, 0.0))).astype(BF16)

    def precompute(blk, carry):
        rows = pl.ds(pl.multiple_of(blk * rows_b, rows_b), rows_b)
        q = (q_scr[rows, :] * dk_scale).reshape(nb, C, HEAD_W)
        k = k_scr[rows, :].reshape(nb, C, HEAD_W)
        v = v_scr[rows, :].reshape(nb, C, HEAD_W)
        sm = sm_ref[0, rows, :].reshape(nb, C, HEAD_W)
        k16 = k.astype(BF16)
        qk_raw = bdot_nt(q.astype(BF16), k16)
        dirs = range(2)
        beta, cum, decay, k_beta, a_mat = [], [], [], [], []
        for d in dirs:
            g = lane_pick(sm, d * N_HEADS + h)
            beta.append(lane_pick(sm, (2 + d) * N_HEADS + h))
            cum_terms = bdot(incl16[d], by_lane_group(split_bf16(jnp.broadcast_to(g, (nb, C, HEAD_W)), 3)))
            cum_d = cum_terms
            for shift in (LANE_GROUP_W, 2 * LANE_GROUP_W, 3 * LANE_GROUP_W):
                cum_d = cum_d + pltpu.roll(cum_terms, shift, 2)
            cum.append(cum_d)
            cum_i = cum_d[:, :, :C]
            cum_j = bdot_nt(group_first_lane16, by_lane_group(split_bf16(cum_d, 3)))
            decay.append(jnp.where(incl[d], jnp.exp(jnp.where(incl[d], cum_i - cum_j, 0.0)), 0.0))
            k_beta.append(k * beta[d])
            a_mat.append(jnp.where(strict[d], bdot_nt(k_beta[d].astype(BF16), k16) * decay[d], 0.0))

        a16 = [a_mat[d].astype(BF16) for d in dirs]
        pw = [bdot(a16[d], a16[d]) for d in dirs]
        r = [-a_mat[d] for d in dirs]
        for step in range(int(math.log2(C)) - 1):
            if step > 0:
                pw16 = [pw[d].astype(BF16) for d in dirs]
                pw = [bdot(pw16[d], pw16[d]) for d in dirs]
            r = [r[d] + pw[d] + bdot(r[d].astype(BF16), pw[d].astype(BF16)) for d in dirs]

        for d in dirs:
            r16 = r[d].astype(BF16)
            e_cum = jnp.exp(cum[d])
            v_beta = v * beta[d]
            kb_dec = k_beta[d] * e_cum
            r_rhs = bdot(r16, jnp.concatenate([v_beta.astype(BF16), kb_dec.astype(BF16)], axis=-1))
            u = v_beta + r_rhs[:, :, :HEAD_W]
            w = kb_dec + r_rhs[:, :, HEAD_W:]
            cum_end = cum[d][:, C - 1:C, :] if d == 0 else cum[d][:, 0:1, :]
            wu16 = jnp.concatenate([w.astype(BF16), u.astype(BF16)], axis=-1)
            kd16 = (k * jnp.exp(cum_end - cum[d])).astype(BF16)
            qk16 = jnp.where(incl[d], qk_raw * decay[d], 0.0).astype(BF16)
            kd_wu = bdot_tn(kd16, wu16)
            qk_wu = bdot(qk16, wu16)
            rows_s = pl.ds(pl.multiple_of(blk * (nb * HEAD_W), nb * HEAD_W), nb * HEAD_W)
            sw_scr[d, rows_s, :] = kd_wu[:, :, :HEAD_W].astype(BF16).reshape(nb * HEAD_W, HEAD_W)
            su_scr[d, rows_s, :] = kd_wu[:, :, HEAD_W:].astype(BF16).reshape(nb * HEAD_W, HEAD_W)
            os_scr[d, rows, :] = (q * e_cum - qk_wu[:, :, :HEAD_W]).astype(BF16).reshape(rows_b, HEAD_W)
            ou_scr[d, rows, :] = qk_wu[:, :, HEAD_W:].reshape(rows_b, HEAD_W)
            gl_scr[d, pl.ds(pl.multiple_of(blk * (nb * 8), nb * 8), nb * 8), :] = jnp.broadcast_to(
                jnp.exp(cum_end), (nb, 8, HEAD_W)).reshape(nb * 8, HEAD_W)
        return carry

    lax.fori_loop(0, n_chunks // nb, precompute, 0)

    def scan_step(t, states):
        new_states = []
        for d in range(2):
            n = t if d == 0 else n_chunks - 1 - t
            rows = pl.ds(pl.multiple_of(n * C, C), C)
            rows_s = pl.ds(pl.multiple_of(n * HEAD_W, HEAD_W), HEAD_W)
            st = states[d]
            st16 = st.astype(BF16)
            o_scr[d, rows, :] = ou_scr[d, rows, :] + _dot(os_scr[d, rows, :], st16)
            gl = gl_scr[d, pl.ds(pl.multiple_of(n * 8, 8), 1), :]
            new_states.append(st * gl + su_scr[d, rows_s, :].astype(F32)
                              - _dot(sw_scr[d, rows_s, :], st16))
        return tuple(new_states)

    zero_state = jnp.zeros((HEAD_W, HEAD_W), F32)
    lax.fori_loop(0, n_chunks, scan_step, (zero_state, zero_state), unroll=2)

    for r0 in range(0, seq, conv_rows):
        sl = slice(r0, r0 + conv_rows)
        o = _rms(o_scr[0, sl, :] + o_scr[1, sl, :], nw_ref[...])
        z = z_ref[0, sl, :]
        o_ref[0, sl, :] = (o * (z * jax.nn.sigmoid(z))).astype(o_ref.dtype)


def _gdn(p_main, gates, conv_w, nw):
    b, s, _ = p_main.shape

    def col(blk):
        return pl.BlockSpec((1, s, HEAD_W), lambda i, h: (i, 0, blk + h))

    def cw(blk):
        return pl.BlockSpec((CONV_K, HEAD_W), lambda i, h: (0, blk + h))

    return pl.pallas_call(
        functools.partial(_gdn_kernel, seq=s),
        grid=(b, N_HEADS),
        in_specs=[
            col(GQ_BLK), col(GK_BLK), col(GV_BLK), col(Z_BLK),
            pl.BlockSpec((1, s, HEAD_W), lambda i, h: (i, 0, 0)),
            cw(0), cw(N_HEADS), cw(2 * N_HEADS),
            pl.BlockSpec((1, HEAD_W), lambda i, h: (0, 0)),
        ],
        out_specs=pl.BlockSpec((1, s, HEAD_W), lambda i, h: (i, 0, h)),
        out_shape=jax.ShapeDtypeStruct((b, s, N_HEADS * HEAD_W), BF16),
        scratch_shapes=[
            pltpu.VMEM((s + 16, HEAD_W), F32),
            pltpu.VMEM((s, HEAD_W), F32),
            pltpu.VMEM((s, HEAD_W), F32),
            pltpu.VMEM((s, HEAD_W), F32),
            pltpu.VMEM((2, (s // GDN_CHUNK) * HEAD_W, HEAD_W), BF16),
            pltpu.VMEM((2, (s // GDN_CHUNK) * HEAD_W, HEAD_W), BF16),
            pltpu.VMEM((2, s, HEAD_W), BF16),
            pltpu.VMEM((2, s, HEAD_W), F32),
            pltpu.VMEM((2, (s // GDN_CHUNK) * 8, HEAD_W), F32),
            pltpu.VMEM((2, s, HEAD_W), F32),
        ],
        compiler_params=_cparams(("parallel", "arbitrary"), vmem=VMEM_LIMIT_LARGE),
        name="gdn",
    )(p_main, p_main, p_main, p_main, gates, conv_w, conv_w, conv_w, nw)


def _merge_kernel(x_ref, oa_ref, og_ref, ga_ref, gg_ref, wpa_ref, wpg_ref, wo_ref, n2w_ref,
                  wr_ref, x2_ref, aff_ref):
    ya = _dot(oa_ref[0], wpa_ref[...])
    yg = _dot(og_ref[0], wpg_ref[...])
    merged = jax.nn.sigmoid(ga_ref[0]) * ya + jax.nn.sigmoid(gg_ref[0]) * yg
    x2 = x_ref[0] + _dot(merged.astype(BF16), wo_ref[...])
    tm = x2.shape[0]
    for j in range(LANE_GROUPS):
        x2_ref[0, pl.ds(j, tm, stride=LANE_GROUPS), :] = x2[:, j * HEAD_W:(j + 1) * HEAD_W]
    h2 = _rms(x2, n2w_ref[...])
    wr = wr_ref[...]
    h_hi = h2.astype(BF16)
    h_lo = (h2 - h_hi.astype(F32)).astype(BF16)
    w_hi = wr.astype(BF16)
    w_lo = (wr - w_hi.astype(F32)).astype(BF16)
    logits = _dot(h_hi, w_hi) + (_dot(h_hi, w_lo) + _dot(h_lo, w_hi))
    lane = lax.broadcasted_iota(jnp.int32, logits.shape, 1)
    valid = lane < N_EXPERTS
    logits = jnp.where(valid, logits, -1e30)
    e = jnp.exp(logits - jnp.max(logits, axis=-1, keepdims=True))
    e = jnp.where(valid, e, 0.0)
    aff_ref[0] = e * (1.0 / jnp.sum(e, axis=-1, keepdims=True))


def _merge(x, oa, og, p_main, wpa, wpg, wo, n2w, wr):
    b, s, d = x.shape
    tm = min(512, s)
    tok = lambda i, m: (i, m, 0)
    const = lambda i, m: (0, 0)
    return pl.pallas_call(
        _merge_kernel,
        grid=(b, s // tm),
        in_specs=[
            pl.BlockSpec((1, tm, d), tok),
            pl.BlockSpec((1, tm, d), tok),
            pl.BlockSpec((1, tm, d), tok),
            pl.BlockSpec((1, tm, d), lambda i, m: (i, m, GATE_A_BLK1024)),
            pl.BlockSpec((1, tm, d), lambda i, m: (i, m, GATE_G_BLK1024)),
            pl.BlockSpec((d, d), const),
            pl.BlockSpec((d, d), const),
            pl.BlockSpec((d, d), const),
            pl.BlockSpec((1, d), const),
            pl.BlockSpec((d, HEAD_W), const),
        ],
        out_specs=[pl.BlockSpec((1, tm * LANE_GROUPS, HEAD_W), tok),
                   pl.BlockSpec((1, tm, HEAD_W), tok)],
        out_shape=[jax.ShapeDtypeStruct((b, s * LANE_GROUPS, HEAD_W), F32),
                   jax.ShapeDtypeStruct((b, s, HEAD_W), F32)],
        compiler_params=_cparams(("parallel", "parallel")),
        name="merge",
    )(x, oa, og, p_main, p_main, wpa, wpg, wo, n2w, wr)


def _route_kernel(aff_ref, idx_ref, val_ref, csm_scr, *, seq, cap):
    blk = 128
    bits = lax.bitcast_convert_type(aff_ref[0], jnp.int32)

    groups = HEAD_W // N_EXPERTS
    packed = aff_ref[0, pl.ds(0, seq // groups, stride=groups), :]
    for g in range(1, groups):
        packed = packed + pltpu.roll(aff_ref[0, pl.ds(g, seq // groups, stride=groups), :],
                                     g * N_EXPERTS, 1)
    bits_packed = lax.bitcast_convert_type(packed, jnp.int32)

    def over_groups(cnt):
        shift = N_EXPERTS
        while shift < HEAD_W:
            cnt = cnt + pltpu.roll(cnt, shift, 1)
            shift *= 2
        return cnt

    def bisect(i, cur):
        cand = cur | jnp.left_shift(jnp.int32(1), 30 - i)
        cnt = over_groups(jnp.sum((bits_packed >= cand).astype(F32), axis=0, keepdims=True))
        return jnp.where(cnt >= cap, cand, cur)

    thr = lax.fori_loop(0, 31, bisect, jnp.zeros((1, HEAD_W), jnp.int32))
    n_gt = jnp.sum((bits > thr).astype(F32), axis=0, keepdims=True)
    n_ties = cap - n_gt

    ri = lax.broadcasted_iota(jnp.int32, (blk, blk), 0)
    ci = lax.broadcasted_iota(jnp.int32, (blk, blk), 1)
    ltri = (ci <= ri).astype(BF16)

    def select(j, carry):
        tie_base, sel_base = carry
        rows = pl.ds(pl.multiple_of(j * blk, blk), blk)
        bj = lax.bitcast_convert_type(aff_ref[0, rows, :], jnp.int32)
        eq = jnp.where(bj == thr, 1.0, 0.0)
        tie_rank = _dot(ltri, eq.astype(BF16)) + tie_base - eq
        sel = jnp.where(bj > thr, 1.0, eq * jnp.where(tie_rank < n_ties, 1.0, 0.0))
        cs = _dot(ltri, sel.astype(BF16)) + sel_base
        csm_scr[rows, :] = cs * sel
        return (tie_base + jnp.sum(eq, axis=0, keepdims=True),
                sel_base + jnp.sum(sel, axis=0, keepdims=True))

    zero = jnp.zeros((1, HEAD_W), F32)
    lax.fori_loop(0, seq // blk, select, (zero, zero))

    rb = min(512, seq)
    slot = lax.broadcasted_iota(jnp.int32, (1, cap), 1).astype(F32) + 1.0
    for e in range(N_EXPERTS):
        def gather_slots(j, carry):
            idx_acc, val_acc = carry
            r0 = pl.multiple_of(j * rb, rb)
            rows = pl.ds(r0, rb)
            onehot = csm_scr[rows, e:e + 1] == slot
            tok = (lax.broadcasted_iota(jnp.int32, (rb, 1), 0) + r0).astype(F32)
            idx_acc = idx_acc + jnp.sum(jnp.where(onehot, tok, 0.0), axis=0, keepdims=True)
            val_acc = val_acc + jnp.sum(jnp.where(onehot, aff_ref[0, rows, e:e + 1], 0.0),
                                        axis=0, keepdims=True)
            return idx_acc, val_acc

        zc = jnp.zeros((1, cap), F32)
        idx_e, val_e = lax.fori_loop(0, seq // rb, gather_slots, (zc, zc))
        idx_ref[0, e:e + 1, :] = idx_e.astype(jnp.int32)
        val_ref[0, e:e + 1, :] = val_e


def _route(aff, cap):
    b, s, _ = aff.shape
    return pl.pallas_call(
        functools.partial(_route_kernel, seq=s, cap=cap),
        grid=(b,),
        in_specs=[pl.BlockSpec((1, s, HEAD_W), lambda i: (i, 0, 0))],
        out_specs=[pl.BlockSpec((1, N_EXPERTS, cap), lambda i: (i, 0, 0)),
                   pl.BlockSpec((1, N_EXPERTS, cap), lambda i: (i, 0, 0))],
        out_shape=[jax.ShapeDtypeStruct((b, N_EXPERTS, cap), jnp.int32),
                   jax.ShapeDtypeStruct((b, N_EXPERTS, cap), F32)],
        scratch_shapes=[pltpu.VMEM((s, HEAD_W), F32)],
        compiler_params=_cparams(("parallel",)),
        name="route",
    )(aff)


def _moe_kernel(idx_ref, val_ref, x2_hbm, n2w_ref, wg_ref, wu_ref, wd_ref, out_hbm,
                x2_v, acc_v, xg_v, xs_v, y_v, yt_v, in_sem, acc_sem, out_sem, *, cap, n_ff):
    G = LANE_GROUPS
    b = pl.program_id(0)
    e = pl.program_id(1)
    f = pl.program_id(2)
    slot0 = (b * N_EXPERTS + e) * cap

    def tile_rows(t):
        return pl.ds(pl.multiple_of(t * G, G), G)

    cp_x = pltpu.make_async_copy(x2_hbm.at[b], x2_v, in_sem)
    cp_a = pltpu.make_async_copy(x2_hbm.at[b], acc_v, acc_sem)

    @pl.when((e == 0) & (f == 0))
    def _():
        cp_x.start()
        cp_a.start()
        cp_x.wait()

    @pl.when(f == 0)
    def _():
        def gather(c, carry):
            xg_v[tile_rows(c), :] = x2_v[tile_rows(idx_ref[slot0 + c]), :]
            return carry

        lax.fori_loop(0, cap, gather, 0, unroll=16)
        groups = [xg_v[pl.ds(j, cap, stride=G), :] for j in range(G)]
        ssq = sum(jnp.sum(g * g, axis=-1, keepdims=True) for g in groups)
        scale = lax.rsqrt(ssq * (1.0 / D_MODEL) + NORM_EPS)
        for j in range(G):
            cols = slice(j * HEAD_W, (j + 1) * HEAD_W)
            xs_v[:, cols] = (groups[j] * scale * n2w_ref[:, cols]).astype(BF16)

    xs = xs_v[...]
    gate = _dot(xs, wg_ref[0])
    up = _dot(xs, wu_ref[0])
    act = (gate * jax.nn.sigmoid(gate) * up).astype(BF16)
    y_part = _dot(act, wd_ref[0])

    @pl.when(f == 0)
    def _():
        y_v[...] = y_part

    @pl.when(f > 0)
    def _():
        y_v[...] += y_part

    @pl.when(f == n_ff - 1)
    def _():
        for j in range(G):
            yt_v[pl.ds(j, cap, stride=G), :] = y_v[:, j * HEAD_W:(j + 1) * HEAD_W]

        @pl.when(e == 0)
        def _():
            cp_a.wait()

        def scatter(g, carry):
            c0 = g * SCATTER_BATCH
            dsts = [tile_rows(idx_ref[slot0 + c0 + j]) for j in range(SCATTER_BATCH)]
            new = [acc_v[dsts[j], :] + yt_v[tile_rows(c0 + j), :] * val_ref[slot0 + c0 + j]
                   for j in range(SCATTER_BATCH)]
            for j in range(SCATTER_BATCH):
                acc_v[dsts[j], :] = new[j]
            return carry

        lax.fori_loop(0, cap // SCATTER_BATCH, scatter, 0)

    @pl.when((e == N_EXPERTS - 1) & (f == n_ff - 1))
    def _():
        cp = pltpu.make_async_copy(acc_v, out_hbm.at[b], out_sem)
        cp.start()
        cp.wait()


def _moe(x2t, idx, val, n2w, wg, wu, wd):
    b, rows, _ = x2t.shape
    d = D_MODEL
    cap = idx.shape[-1]
    tf = 1024
    n_ff = EXPERT_FF // tf
    grid_spec = pltpu.PrefetchScalarGridSpec(
        num_scalar_prefetch=2,
        grid=(b, N_EXPERTS, n_ff),
        in_specs=[
            pl.BlockSpec(memory_space=pl.ANY),
            pl.BlockSpec((1, d), lambda i, e, f, *_: (0, 0)),
            pl.BlockSpec((1, d, tf), lambda i, e, f, *_: (e, 0, f)),
            pl.BlockSpec((1, d, tf), lambda i, e, f, *_: (e, 0, f)),
            pl.BlockSpec((1, tf, d), lambda i, e, f, *_: (e, f, 0)),
        ],
        out_specs=pl.BlockSpec(memory_space=pl.ANY),
        scratch_shapes=[
            pltpu.VMEM((rows, HEAD_W), F32),
            pltpu.VMEM((rows, HEAD_W), F32),
            pltpu.VMEM((cap * LANE_GROUPS, HEAD_W), F32),
            pltpu.VMEM((cap, d), BF16),
            pltpu.VMEM((cap, d), F32),
            pltpu.VMEM((cap * LANE_GROUPS, HEAD_W), F32),
            pltpu.SemaphoreType.DMA,
            pltpu.SemaphoreType.DMA,
            pltpu.SemaphoreType.DMA,
        ],
    )
    return pl.pallas_call(
        functools.partial(_moe_kernel, cap=cap, n_ff=n_ff),
        grid_spec=grid_spec,
        out_shape=jax.ShapeDtypeStruct(x2t.shape, F32),
        compiler_params=_cparams(("arbitrary", "arbitrary", "arbitrary"), vmem=VMEM_LIMIT_LARGE),
        name="moe",
    )(idx.reshape(-1), val.reshape(-1), x2t, n2w, wg, wu, wd)


def _final_norm_kernel(xt_ref, w_ref, o_ref):
    tm = o_ref.shape[1]
    groups = [xt_ref[0, pl.ds(j, tm, stride=LANE_GROUPS), :] for j in range(LANE_GROUPS)]
    ssq = sum(jnp.sum(g * g, axis=-1, keepdims=True) for g in groups)
    scale = lax.rsqrt(ssq * (1.0 / D_MODEL) + NORM_EPS)
    for j in range(LANE_GROUPS):
        cols = slice(j * HEAD_W, (j + 1) * HEAD_W)
        o_ref[0, :, cols] = groups[j] * scale * w_ref[:, cols]


def _final_norm(xt, w):
    b, rows, _ = xt.shape
    s = rows // LANE_GROUPS
    tm = min(512, s)
    return pl.pallas_call(
        _final_norm_kernel,
        grid=(b, s // tm),
        in_specs=[pl.BlockSpec((1, tm * LANE_GROUPS, HEAD_W), lambda i, m: (i, m, 0)),
                  pl.BlockSpec((1, D_MODEL), lambda i, m: (0, 0))],
        out_specs=pl.BlockSpec((1, tm, D_MODEL), lambda i, m: (i, m, 0)),
        out_shape=jax.ShapeDtypeStruct((b, s, D_MODEL), F32),
        compiler_params=_cparams(("parallel", "parallel")),
        name="final_norm",
    )(xt, w)


def _rope_tables(seq):
    inv = ROPE_THETA ** (-jnp.arange(0, DA_DH, 2, dtype=F32) / DA_DH)
    ang = jnp.arange(seq, dtype=F32)[:, None] * inv[None, :]
    cos, sin = jnp.cos(ang), jnp.sin(ang)
    return (jnp.concatenate([cos, cos, cos, cos], axis=-1),
            jnp.concatenate([-sin, sin, -sin, sin], axis=-1))


def kernel(x, norm1_w, w_in, conv_w, a_log_fwd, dt_bias_fwd, a_log_bwd, dt_bias_bwd, gdn_norm_w,
           lambda_q1, lambda_k1, lambda_q2, lambda_k2, subln_w, w_proj_attn, w_proj_gdn, w_out,
           norm2_w, w_router, w_gate, w_up, w_down, norm_f_w):
    assert w_in.shape[0] == 1, "single-layer block"
    b, s, d = x.shape
    cap = CAP_FACTOR * s // N_EXPERTS
    w = w_in[0]
    small0 = 7 * D_MODEL
    small_w = 4 * N_HEADS
    w_main = jnp.concatenate([w[:, :small0], w[:, small0 + small_w:]], axis=1).astype(BF16)
    w_small = jnp.pad(w[:, small0:small0 + small_w], ((0, 0), (0, HEAD_W - small_w))).astype(BF16)

    pad_lanes = jnp.zeros((HEAD_W - 2 * N_HEADS,), F32)
    a_log_lanes = jnp.concatenate([a_log_fwd[0], a_log_bwd[0], pad_lanes]).reshape(1, HEAD_W)
    dt_bias_lanes = jnp.concatenate([dt_bias_fwd[0], dt_bias_bwd[0], pad_lanes]).reshape(1, HEAD_W)
    p_main, gates = _in_proj(x, norm1_w, w_main, w_small, a_log_lanes, dt_bias_lanes)

    cos, sin_signed = _rope_tables(s)
    oa = _attention(p_main, cos, sin_signed, lambda_q1, lambda_k1, lambda_q2, lambda_k2, subln_w)
    og = _gdn(p_main, gates, conv_w[0], gdn_norm_w)

    w_r = jnp.pad(w_router[0], ((0, 0), (0, HEAD_W - N_EXPERTS)))
    x2, aff = _merge(x, oa, og, p_main, w_proj_attn[0].astype(BF16), w_proj_gdn[0].astype(BF16),
                     w_out[0].astype(BF16), norm2_w, w_r)
    idx, val = _route(aff, cap)
    y = _moe(x2, idx, val, norm2_w, w_gate[0].astype(BF16), w_up[0].astype(BF16),
             w_down[0].astype(BF16))
    return _final_norm(y, norm_f_w.reshape(1, d))
```
